```python
import math
import jax, jax.numpy as jnp
from jax import lax
import numpy as np

D_MODEL = 1024
BATCH = 4
SEQ = 8192
DEPTH = 1

CHUNK = 64
Q_BLOCK = 128
EPS = 1e-6

MLA_HEADS = 8
QK_NOPE_DIM = 64
QK_ROPE_DIM = 32
V_HEAD_DIM = 128
Q_LORA_RANK = 384
KV_LORA_RANK = 256
MLA_WIDTH = MLA_HEADS * V_HEAD_DIM
ROPE_THETA = 10000.0

GMLP_GROUPS = 8
GMLP_GROUP_DIM = 128
GMLP_WIDTH = GMLP_GROUPS * GMLP_GROUP_DIM
SPATIAL_BLOCK = 128

D_MIX = MLA_WIDTH + GMLP_WIDTH

IN_SPLITS = (
    Q_LORA_RANK,
    KV_LORA_RANK,
    QK_ROPE_DIM,
    MLA_WIDTH,
    GMLP_WIDTH,
    GMLP_WIDTH,
    GMLP_WIDTH,
)
D_IN = sum(IN_SPLITS)

kernel_name = "hybrid_gmlp_mla_parallel_heads"


def rms_norm(x, g):
    xf = x.astype(jnp.float32)
    y = xf * lax.rsqrt(jnp.mean(xf * xf, axis=-1, keepdims=True) + EPS)
    return (y * g.astype(jnp.float32)).astype(x.dtype)


def layer_norm(x, g, b):
    xf = x.astype(jnp.float32)
    mu = jnp.mean(xf, axis=-1, keepdims=True)
    var = jnp.mean(jnp.square(xf - mu), axis=-1, keepdims=True)
    y = (xf - mu) * lax.rsqrt(var + EPS)
    return (y * g.astype(jnp.float32) + b.astype(jnp.float32)).astype(x.dtype)


def rope_tables(seq):
    pos = jnp.arange(seq, dtype=jnp.float32)
    inv_freq = ROPE_THETA ** (-jnp.arange(0, QK_ROPE_DIM, 2, dtype=jnp.float32) / QK_ROPE_DIM)
    ang = pos[:, None] * inv_freq[None, :]
    return jnp.cos(ang), jnp.sin(ang)


def apply_rope(x, cos, sin):
    xf = x.astype(jnp.float32)
    x1, x2 = jnp.split(xf, 2, axis=-1)
    out = jnp.concatenate([x1 * cos - x2 * sin, x1 * sin + x2 * cos], axis=-1)
    return out.astype(x.dtype)


def mla_branch(q_lat, kv_lat, k_rope_raw, q_norm_g, w_uq, kv_norm_g, w_ukv):
    b, s, _ = q_lat.shape
    cos, sin = rope_tables(s)
    q = jnp.einsum("bsr,rd->bsd", rms_norm(q_lat, q_norm_g), w_uq)
    q = q.reshape(b, s, MLA_HEADS, QK_NOPE_DIM + QK_ROPE_DIM)
    q_nope, q_rope = q[..., :QK_NOPE_DIM], q[..., QK_NOPE_DIM:]
    q_rope = apply_rope(q_rope, cos[None, :, None, :], sin[None, :, None, :])
    k_rope = apply_rope(k_rope_raw, cos[None], sin[None])
    kv = jnp.einsum("bsr,rd->bsd", rms_norm(kv_lat, kv_norm_g), w_ukv)
    kv = kv.reshape(b, s, MLA_HEADS, QK_NOPE_DIM + V_HEAD_DIM)
    k_nope, v = kv[..., :QK_NOPE_DIM], kv[..., QK_NOPE_DIM:]

    scale = 1.0 / math.sqrt(QK_NOPE_DIM + QK_ROPE_DIM)
    n_blocks = s // Q_BLOCK
    k_chunk = jnp.arange(s) // CHUNK
    qn_blocks = q_nope.reshape(b, n_blocks, Q_BLOCK, MLA_HEADS, QK_NOPE_DIM).transpose(1, 0, 2, 3, 4)
    qr_blocks = q_rope.reshape(b, n_blocks, Q_BLOCK, MLA_HEADS, QK_ROPE_DIM).transpose(1, 0, 2, 3, 4)
    starts = jnp.arange(n_blocks, dtype=jnp.int32) * Q_BLOCK

    def attend(args):
        qn, qr, start = args
        sc = (jnp.einsum("bqhd,bkhd->bhqk", qn, k_nope)
              + jnp.einsum("bqhr,bkr->bhqk", qr, k_rope)).astype(jnp.float32) * scale
        q_chunk = (start + jnp.arange(Q_BLOCK)) // CHUNK
        mask = k_chunk[None, :] <= q_chunk[:, None]
        sc = jnp.where(mask[None, None], sc, jnp.float32(-1e30))
        p = jax.nn.softmax(sc, axis=-1).astype(v.dtype)
        return jnp.einsum("bhqk,bkhd->bqhd", p, v)

    out = lax.map(attend, (qn_blocks, qr_blocks, starts))
    return out.transpose(1, 0, 2, 3, 4).reshape(b, s, MLA_WIDTH)


def gmlp_branch(u, v, ln_g, ln_b, w_spatial, b_spatial):
    b, s, _ = u.shape
    v = layer_norm(v, ln_g, ln_b)
    nc = s // SPATIAL_BLOCK
    v = v.reshape(b, nc, SPATIAL_BLOCK, GMLP_GROUPS, GMLP_GROUP_DIM)
    t_idx = jnp.arange(SPATIAL_BLOCK) // CHUNK
    mask = (t_idx[None, :] <= t_idx[:, None]).astype(w_spatial.dtype)
    ws = w_spatial * mask[None]
    mixed = jnp.einsum("gts,bcsgd->bctgd", ws, v) + b_spatial.T[None, None, :, :, None]
    return u * mixed.reshape(b, s, GMLP_WIDTH)


def setup_inputs(seed: int = 0) -> dict:
    key = jax.random.key(seed)
    ks = jax.random.split(key, 16)
    f32 = jnp.float32

    def normal(k, shape, scale):
        return jax.random.normal(k, shape, f32) * scale

    def gain(k, n):
        return 1.0 + 0.02 * jax.random.normal(k, (n,), f32)

    return {
        "x": jax.random.normal(ks[0], (BATCH, SEQ, D_MODEL), f32),
        "norm_in_g": gain(ks[1], D_MODEL),
        "w_in": normal(ks[2], (D_MODEL, D_IN), D_MODEL ** -0.5),
        "q_norm_g": gain(ks[3], Q_LORA_RANK),
        "w_uq": normal(ks[4], (Q_LORA_RANK, MLA_HEADS * (QK_NOPE_DIM + QK_ROPE_DIM)), Q_LORA_RANK ** -0.5),
        "kv_norm_g": gain(ks[5], KV_LORA_RANK),
        "w_ukv": normal(ks[6], (KV_LORA_RANK, MLA_HEADS * (QK_NOPE_DIM + V_HEAD_DIM)), KV_LORA_RANK ** -0.5),
        "gmlp_ln_g": gain(ks[7], GMLP_WIDTH),
        "gmlp_ln_b": normal(ks[8], (GMLP_WIDTH,), 0.02),
        "w_spatial": normal(ks[9], (GMLP_GROUPS, SPATIAL_BLOCK, SPATIAL_BLOCK), SPATIAL_BLOCK ** -0.5),
        "b_spatial": 1.0 + normal(ks[10], (GMLP_GROUPS, SPATIAL_BLOCK), 0.02),
        "out_norm_mla_g": gain(ks[11], MLA_WIDTH),
        "out_norm_gmlp_g": gain(ks[12], GMLP_WIDTH),
        "w_out": normal(ks[13], (D_MIX, D_MODEL), D_MIX ** -0.5),
        "final_norm_g": gain(ks[14], D_MODEL),
    }


def reference(x, norm_in_g, w_in, q_norm_g, w_uq, kv_norm_g, w_ukv,
              gmlp_ln_g, gmlp_ln_b, w_spatial, b_spatial,
              out_norm_mla_g, out_norm_gmlp_g, w_out, final_norm_g):
    h = x
    bounds = list(np.cumsum(IN_SPLITS)[:-1])
    for _ in range(DEPTH):
        y = rms_norm(h, norm_in_g)
        proj = jnp.einsum("bsd,de->bse", y, w_in)
        q_lat, kv_lat, k_rope, gate_mla, u, v, gate_gmlp = jnp.split(proj, bounds, axis=-1)

        o_mla = mla_branch(q_lat, kv_lat, k_rope, q_norm_g, w_uq, kv_norm_g, w_ukv)
        o_gmlp = gmlp_branch(jax.nn.gelu(u, approximate=False), jax.nn.gelu(v, approximate=False),
                             gmlp_ln_g, gmlp_ln_b, w_spatial, b_spatial)

        o_mla = rms_norm(o_mla, out_norm_mla_g) * jax.nn.silu(gate_mla)
        o_gmlp = rms_norm(o_gmlp, out_norm_gmlp_g) * jax.nn.silu(gate_gmlp)
        mixed = jnp.concatenate([o_mla, o_gmlp], axis=-1)
        h = h + jnp.einsum("bse,ed->bsd", mixed, w_out)
    return rms_norm(h, final_norm_g)
```

```python
import functools
import math

import jax
import jax.numpy as jnp
from jax import lax
from jax.experimental import pallas as pl
from jax.experimental.pallas import tpu as pltpu

D_MODEL = 1024
CHUNK = 64
EPS = 1e-6

HEADS = 8
NOPE = 64
ROPE = 32
DV = 128
DQK = 128
Q_RANK = 384
KV_RANK = 256
MLA_W = HEADS * DV
ROPE_THETA = 10000.0

GROUPS = 8
GDIM = 128
GMLP_W = GROUPS * GDIM
SBLOCK = 128

C_QLAT = 0
C_KVLAT = C_QLAT + Q_RANK
C_KR = C_KVLAT + KV_RANK
C_KRROT = C_KR + DQK
C_GMLA = C_KRROT + DQK
C_U = C_GMLA + MLA_W
C_V = C_U + GMLP_W
C_GG = C_V + GMLP_W
D_IN_PACKED = C_GG + GMLP_W

V7X_VMEM_BYTES = 64 * 1024 * 1024
VMEM_LIMIT = 56 * 1024 * 1024

TM_PROJ = 256
TQ = 512
TK = 512
TM_OUT = 512

NEG_BIG = -1e30


def _rms(x, g):
    return x * lax.rsqrt(jnp.mean(x * x, axis=-1, keepdims=True) + EPS) * g


def _dot(a, b):
    return jnp.dot(a, b, preferred_element_type=jnp.float32)


def _silu(x):
    return x * (1.0 / (1.0 + jnp.exp(-x)))


def _gelu(x):
    return 0.5 * x * (1.0 + lax.erf(x * (1.0 / math.sqrt(2.0))))


def _proj_kernel(x_ref, cos_ref, sin_ref, ng_ref, win_ref, qg_ref, wuq_ref, wuqr_ref,
                 kvg_ref, wkn_ref, wv_ref, lng_ref, lnb_ref, ws_ref, bs_ref, ogg_ref,
                 q_ref, k_ref, v_ref, gate_ref, og_ref):
    x = x_ref[...]
    y = _rms(x, ng_ref[...]).astype(jnp.bfloat16)

    cos = cos_ref[...]
    sin = sin_ref[...]
    scale = 1.0 / math.sqrt(NOPE + ROPE)

    lat = _dot(y, win_ref[:, C_QLAT:C_GMLA])
    qn = _rms(lat[:, C_QLAT:C_KVLAT], qg_ref[...]).astype(jnp.bfloat16)
    kvn = _rms(lat[:, C_KVLAT:C_KR], kvg_ref[...]).astype(jnp.bfloat16)
    kr = lat[:, C_KR:C_KRROT] * cos + lat[:, C_KRROT:C_GMLA] * sin

    q_all = _dot(qn, wuq_ref[...])
    q_rot = _dot(qn, wuqr_ref[...])
    k_all = _dot(kvn, wkn_ref[...])
    v_all = _dot(kvn, wv_ref[...])
    for h in range(HEADS):
        sl = slice(h * DQK, (h + 1) * DQK)
        qh = (q_all[:, sl] * cos + q_rot[:, sl] * sin) * scale
        q_ref[0, h] = qh.astype(jnp.bfloat16)
        k_ref[0, h] = (k_all[:, sl] + kr).astype(jnp.bfloat16)
        v_ref[0, h] = v_all[:, sl].astype(jnp.bfloat16)

    gate_ref[...] = _silu(_dot(y, win_ref[:, C_GMLA:C_U])).astype(jnp.bfloat16)

    u = _gelu(_dot(y, win_ref[:, C_U:C_V]))
    v = _gelu(_dot(y, win_ref[:, C_V:C_GG]))
    mu = jnp.mean(v, axis=-1, keepdims=True)
    vc = v - mu
    var = jnp.mean(vc * vc, axis=-1, keepdims=True)
    vln = (vc * lax.rsqrt(var + EPS) * lng_ref[...] + lnb_ref[...]).astype(jnp.bfloat16)

    tc = lax.broadcasted_iota(jnp.int32, (SBLOCK, SBLOCK), 0) // CHUNK
    sc = lax.broadcasted_iota(jnp.int32, (SBLOCK, SBLOCK), 1) // CHUNK
    ws = [jnp.where(sc <= tc, ws_ref[g], 0.0).astype(jnp.bfloat16) for g in range(GROUPS)]

    tm = x.shape[0]
    rows = []
    for c in range(tm // SBLOCK):
        cols = []
        for g in range(GROUPS):
            vb = vln[c * SBLOCK:(c + 1) * SBLOCK, g * GDIM:(g + 1) * GDIM]
            cols.append(_dot(ws[g], vb) + bs_ref[g])
        rows.append(jnp.concatenate(cols, axis=1))
    mixed = jnp.concatenate(rows, axis=0)
    og = _rms(u * mixed, ogg_ref[...])
    og_ref[...] = (og * _silu(_dot(y, win_ref[:, C_GG:D_IN_PACKED]))).astype(jnp.bfloat16)


def _const_spec(shape):
    nd = len(shape)
    return pl.BlockSpec(shape, lambda i, _nd=nd: (0,) * _nd)


def _proj_call(x2, cos_t, sin_t, ng, win, qg, wuq, wuqr, kvg, wkn, wv, lng, lnb, ws, bs, ogg,
               batch, seq):
    n = x2.shape[0]
    tm = TM_PROJ
    tiles_per_seq = seq // tm
    row_spec = pl.BlockSpec((tm, D_MODEL), lambda i: (i, 0))
    tab_spec = pl.BlockSpec((tm, DQK), lambda i: (i % tiles_per_seq, 0))
    head_spec = pl.BlockSpec((1, HEADS, tm, DQK),
                             lambda i: (i // tiles_per_seq, 0, i % tiles_per_seq, 0))
    head_shape = jax.ShapeDtypeStruct((batch, HEADS, seq, DQK), jnp.bfloat16)
    row_shape = jax.ShapeDtypeStruct((n, D_MODEL), jnp.bfloat16)
    consts = (ng, win, qg, wuq, wuqr, kvg, wkn, wv, lng, lnb, ws, bs, ogg)
    return pl.pallas_call(
        _proj_kernel,
        grid=(n // tm,),
        in_specs=[row_spec, tab_spec, tab_spec] + [_const_spec(c.shape) for c in consts],
        out_specs=[head_spec, head_spec, head_spec, row_spec, row_spec],
        out_shape=[head_shape, head_shape, head_shape, row_shape, row_shape],
        compiler_params=pltpu.CompilerParams(
            dimension_semantics=("parallel",), vmem_limit_bytes=VMEM_LIMIT),
        name="proj_gmlp",
    )(x2, cos_t, sin_t, *consts)


def _attn_kernel(q_ref, k_ref, v_ref, o_ref, m_ref, l_ref, acc_ref):
    qi = pl.program_id(2)
    q = q_ref[0, 0]
    m_ref[...] = jnp.full_like(m_ref, NEG_BIG)
    l_ref[...] = jnp.zeros_like(l_ref)
    acc_ref[...] = jnp.zeros_like(acc_ref)

    def step(t, masked):
        start = pl.multiple_of(t * TK, TK)
        k = k_ref[0, 0, pl.ds(start, TK), :]
        v = v_ref[0, 0, pl.ds(start, TK), :]
        s = lax.dot_general(q, k, (((1,), (1,)), ((), ())),
                            preferred_element_type=jnp.float32)
        if masked:
            rc = lax.broadcasted_iota(jnp.int32, (TQ, TK), 0) // CHUNK
            cc = lax.broadcasted_iota(jnp.int32, (TQ, TK), 1) // CHUNK
            s = jnp.where(cc <= rc, s, NEG_BIG)
        m_prev = m_ref[...]
        m_new = jnp.maximum(m_prev, jnp.max(s, axis=-1, keepdims=True))
        alpha = jnp.exp(m_prev - m_new)
        p = jnp.exp(s - m_new)
        l_ref[...] = alpha * l_ref[...] + jnp.sum(p, axis=-1, keepdims=True)
        acc_ref[...] = alpha * acc_ref[...] + _dot(p.astype(jnp.bfloat16), v)
        m_ref[...] = m_new

    def body(t, carry):
        step(t, False)
        return carry

    lax.fori_loop(0, qi, body, 0)
    step(qi, True)
    o_ref[...] = (acc_ref[...] / l_ref[...]).astype(o_ref.dtype)


def _attn_call(q, k, v, batch, seq):
    assert TQ == TK
    return pl.pallas_call(
        _attn_kernel,
        grid=(batch, HEADS, seq // TQ),
        in_specs=[
            pl.BlockSpec((1, 1, TQ, DQK), lambda b, h, i: (b, h, i, 0)),
            pl.BlockSpec((1, 1, seq, DQK), lambda b, h, i: (b, h, 0, 0)),
            pl.BlockSpec((1, 1, seq, DV), lambda b, h, i: (b, h, 0, 0)),
        ],
        out_specs=pl.BlockSpec((TQ, DV), lambda b, h, i: (b * (seq // TQ) + i, h)),
        out_shape=jax.ShapeDtypeStruct((batch * seq, MLA_W), jnp.bfloat16),
        scratch_shapes=[
            pltpu.VMEM((TQ, 1), jnp.float32),
            pltpu.VMEM((TQ, 1), jnp.float32),
            pltpu.VMEM((TQ, DV), jnp.float32),
        ],
        compiler_params=pltpu.CompilerParams(
            dimension_semantics=("parallel", "parallel", "arbitrary"),
            vmem_limit_bytes=VMEM_LIMIT),
        name="mla_attn",
    )(q, k, v)


def _out_kernel(x_ref, o_ref, gate_ref, og_ref, omg_ref, wo_ref, fg_ref, out_ref):
    om = _rms(o_ref[...].astype(jnp.float32), omg_ref[...]) * gate_ref[...].astype(jnp.float32)
    h = x_ref[...] + _dot(om.astype(jnp.bfloat16), wo_ref[0:MLA_W, :]) \
        + _dot(og_ref[...], wo_ref[MLA_W:, :])
    out_ref[...] = _rms(h, fg_ref[...])


def _out_call(x2, o_mla, gate, og, omg, wo, fg):
    n = x2.shape[0]
    tm = TM_OUT
    row_spec = pl.BlockSpec((tm, D_MODEL), lambda i: (i, 0))
    return pl.pallas_call(
        _out_kernel,
        grid=(n // tm,),
        in_specs=[row_spec, row_spec, row_spec, row_spec,
                  _const_spec(omg.shape), _const_spec(wo.shape), _const_spec(fg.shape)],
        out_specs=row_spec,
        out_shape=jax.ShapeDtypeStruct((n, D_MODEL), jnp.float32),
        compiler_params=pltpu.CompilerParams(
            dimension_semantics=("parallel",), vmem_limit_bytes=VMEM_LIMIT),
        name="out_proj",
    )(x2, o_mla, gate, og, omg, wo, fg)


def _rope_tables(seq):
    pos = jnp.arange(seq, dtype=jnp.float32)
    inv_freq = ROPE_THETA ** (-jnp.arange(0, ROPE, 2, dtype=jnp.float32) / ROPE)
    ang = pos[:, None] * inv_freq[None, :]
    cos, sin = jnp.cos(ang), jnp.sin(ang)
    ones = jnp.ones((seq, NOPE), jnp.float32)
    zeros_n = jnp.zeros((seq, NOPE), jnp.float32)
    pad = jnp.zeros((seq, DQK - NOPE - ROPE), jnp.float32)
    cos_t = jnp.concatenate([ones, cos, cos, pad], axis=1)
    sin_t = jnp.concatenate([zeros_n, sin, sin, pad], axis=1)
    return cos_t, sin_t


def _rot_half_cols(w):
    half = ROPE // 2
    return jnp.concatenate([-w[..., half:], w[..., :half]], axis=-1)


def _pack_weights(w_in, w_uq, w_ukv):
    bf = jnp.bfloat16
    d = w_in.shape[0]
    kr_w = w_in[:, 640:672]
    z64 = jnp.zeros((d, NOPE), w_in.dtype)
    z32 = jnp.zeros((d, DQK - NOPE - ROPE), w_in.dtype)
    win = jnp.concatenate([
        w_in[:, 0:640],
        z64, kr_w, z32,
        z64, _rot_half_cols(kr_w), z32,
        w_in[:, 672:],
    ], axis=1).astype(bf)
    wq = w_uq.reshape(Q_RANK, HEADS, NOPE + ROPE)
    zq = jnp.zeros((Q_RANK, HEADS, DQK - NOPE - ROPE), w_uq.dtype)
    wuq = jnp.concatenate([wq, zq], axis=-1).reshape(Q_RANK, HEADS * DQK).astype(bf)
    wuqr = jnp.concatenate([jnp.zeros((Q_RANK, HEADS, NOPE), w_uq.dtype),
                            _rot_half_cols(wq[..., NOPE:]), zq],
                           axis=-1).reshape(Q_RANK, HEADS * DQK).astype(bf)
    wkv = w_ukv.reshape(KV_RANK, HEADS, NOPE + DV)
    wkn = jnp.concatenate([wkv[..., :NOPE], jnp.zeros((KV_RANK, HEADS, DQK - NOPE), w_ukv.dtype)],
                          axis=-1).reshape(KV_RANK, HEADS * DQK).astype(bf)
    wv = wkv[..., NOPE:].reshape(KV_RANK, HEADS * DV).astype(bf)
    return win, wuq, wuqr, wkn, wv


def kernel(x, norm_in_g, w_in, q_norm_g, w_uq, kv_norm_g, w_ukv, gmlp_ln_g, gmlp_ln_b,
           w_spatial, b_spatial, out_norm_mla_g, out_norm_gmlp_g, w_out, final_norm_g):
    batch, seq, d = x.shape
    n = batch * seq
    x2 = x.reshape(n, d)
    row = lambda a: a.reshape(1, -1).astype(jnp.float32)

    win, wuq, wuqr, wkn, wv = _pack_weights(w_in, w_uq, w_ukv)
    cos_t, sin_t = _rope_tables(seq)
    ws = w_spatial.astype(jnp.float32)
    bs =jnp.broadcast_to(b_spatial[:, :, None], (GROUPS, SBLOCK, GDIM)).astype(jnp.float32)

    q, k, v, gate, og = _proj_call(
        x2, cos_t, sin_t, row(norm_in_g), win, row(q_norm_g), wuq, wuqr, row(kv_norm_g),
        wkn, wv, row(gmlp_ln_g), row(gmlp_ln_b), ws, bs, row(out_norm_gmlp_g), batch, seq)
    o_mla = _attn_call(q, k, v, batch, seq)
    out = _out_call(x2, o_mla, gate, og, row(out_norm_mla_g), w_out.astype(jnp.bfloat16),
                    row(final_norm_g))
    return out.reshape(batch, seq, d)
```

```python
import functools
import math

import jax
import jax.numpy as jnp
from jax import lax
from jax.experimental import pallas as pl
from jax.experimental.pallas import tpu as pltpu

D_MODEL = 1024
CHUNK = 64
EPS = 1e-6

HEADS = 8
NOPE = 64
ROPE = 32
DV = 128
DQK = 128
Q_RANK = 384
KV_RANK = 256
MLA_W = HEADS * DV
ROPE_THETA = 10000.0

GROUPS = 8
GDIM = 128
GMLP_W = GROUPS * GDIM
SBLOCK = 128

C_QLAT = 0
C_KVLAT = C_QLAT + Q_RANK
C_KR = C_KVLAT + KV_RANK
C_KRROT = C_KR + DQK
C_GMLA = C_KRROT + DQK
C_U = C_GMLA + MLA_W
C_V = C_U + GMLP_W
C_GG = C_V + GMLP_W
D_IN_PACKED = C_GG + GMLP_W

V7X_VMEM_BYTES = 64 * 1024 * 1024
VMEM_LIMIT = 56 * 1024 * 1024

TM_PROJ = 256
TQ = 512
TK = 512
HB = 2
TM_OUT = 512

NEG_BIG = -1e30


def _rms(x, g):
    return x * lax.rsqrt(jnp.mean(x * x, axis=-1, keepdims=True) + EPS) * g


def _dot(a, b):
    return jnp.dot(a, b, preferred_element_type=jnp.float32)


def _silu(x):
    return x * (1.0 / (1.0 + jnp.exp(-x)))


def _gelu(x):
    return 0.5 * x * (1.0 + lax.erf(x * (1.0 / math.sqrt(2.0))))


def _proj_kernel(x_ref, cos_ref, sin_ref, ng_ref, win_ref, qg_ref, wuq_ref, wuqr_ref,
                 kvg_ref, wkn_ref, wv_ref, lng_ref, lnb_ref, ws_ref, bs_ref, ogg_ref,
                 qt_ref, k_ref, vt_ref, gate_ref, og_ref):
    x = x_ref[...]
    y = _rms(x, ng_ref[...]).astype(jnp.bfloat16)

    cos = cos_ref[...]
    sin = sin_ref[...]
    scale = math.log2(math.e) / math.sqrt(NOPE + ROPE)

    lat = _dot(y, win_ref[:, C_QLAT:C_GMLA])
    qn = _rms(lat[:, C_QLAT:C_KVLAT], qg_ref[...]).astype(jnp.bfloat16)
    kvn = _rms(lat[:, C_KVLAT:C_KR], kvg_ref[...]).astype(jnp.bfloat16)
    kr = lat[:, C_KR:C_KRROT] * cos + lat[:, C_KRROT:C_GMLA] * sin

    q_all = _dot(qn, wuq_ref[...])
    q_rot = _dot(qn, wuqr_ref[...])
    k_all = _dot(kvn, wkn_ref[...])
    v_all = _dot(kvn, wv_ref[...])
    for h in range(HEADS):
        sl = slice(h * DQK, (h + 1) * DQK)
        qh = (q_all[:, sl] * cos + q_rot[:, sl] * sin) * scale
        qt_ref[0, h] = qh.T.astype(jnp.bfloat16)
        k_ref[0, h] = (k_all[:, sl] + kr).astype(jnp.bfloat16)
        vt_ref[0, h] = v_all[:, sl].T.astype(jnp.bfloat16)

    gate_ref[...] = _silu(_dot(y, win_ref[:, C_GMLA:C_U])).astype(jnp.bfloat16)

    u = _gelu(_dot(y, win_ref[:, C_U:C_V]))
    v = _gelu(_dot(y, win_ref[:, C_V:C_GG]))
    mu = jnp.mean(v, axis=-1, keepdims=True)
    vc = v - mu
    var = jnp.mean(vc * vc, axis=-1, keepdims=True)
    vln = (vc * lax.rsqrt(var + EPS) * lng_ref[...] + lnb_ref[...]).astype(jnp.bfloat16)

    tc = lax.broadcasted_iota(jnp.int32, (SBLOCK, SBLOCK), 0) // CHUNK
    sc = lax.broadcasted_iota(jnp.int32, (SBLOCK, SBLOCK), 1) // CHUNK
    ws = [jnp.where(sc <= tc, ws_ref[g], 0.0).astype(jnp.bfloat16) for g in range(GROUPS)]

    tm = x.shape[0]
    rows = []
    for c in range(tm // SBLOCK):
        cols = []
        for g in range(GROUPS):
            vb = vln[c * SBLOCK:(c + 1) * SBLOCK, g * GDIM:(g + 1) * GDIM]
            cols.append(_dot(ws[g], vb) + bs_ref[g])
        rows.append(jnp.concatenate(cols, axis=1))
    mixed = jnp.concatenate(rows, axis=0)
    og = _rms(u * mixed, ogg_ref[...])
    og_ref[...] = (og * _silu(_dot(y, win_ref[:, C_GG:D_IN_PACKED]))).astype(jnp.bfloat16)


def _const_spec(shape):
    nd = len(shape)
    return pl.BlockSpec(shape, lambda i, _nd=nd: (0,) * _nd)


def _proj_call(x2, cos_t, sin_t, ng, win, qg, wuq, wuqr, kvg, wkn, wv, lng, lnb, ws, bs, ogg,
               batch, seq):
    n = x2.shape[0]
    tm = TM_PROJ
    tiles_per_seq = seq // tm
    row_spec = pl.BlockSpec((tm, D_MODEL), lambda i: (i, 0))
    tab_spec = pl.BlockSpec((tm, DQK), lambda i: (i % tiles_per_seq, 0))
    head_spec = pl.BlockSpec((1, HEADS, tm, DQK),
                             lambda i: (i // tiles_per_seq, 0, i % tiles_per_seq, 0))
    head_shape = jax.ShapeDtypeStruct((batch, HEADS, seq, DQK), jnp.bfloat16)
    headt_spec = pl.BlockSpec((1, HEADS, DQK, tm),
                              lambda i: (i // tiles_per_seq, 0, 0, i % tiles_per_seq))
    headt_shape = jax.ShapeDtypeStruct((batch, HEADS, DQK, seq), jnp.bfloat16)
    row_shape = jax.ShapeDtypeStruct((n, D_MODEL), jnp.bfloat16)
    consts = (ng, win, qg, wuq, wuqr, kvg, wkn, wv, lng, lnb, ws, bs, ogg)
    return pl.pallas_call(
        _proj_kernel,
        grid=(n // tm,),
        in_specs=[row_spec, tab_spec, tab_spec] + [_const_spec(c.shape) for c in consts],
        out_specs=[headt_spec, head_spec, headt_spec, row_spec, row_spec],
        out_shape=[headt_shape, head_shape, headt_shape, row_shape, row_shape],
        compiler_params=pltpu.CompilerParams(
            dimension_semantics=("parallel",), vmem_limit_bytes=VMEM_LIMIT),
        name="proj_gmlp",
    )(x2, cos_t, sin_t, *consts)


def _attn_kernel(qt_ref, k_ref, vt_ref, o_ref, acc_ref):
    qi = pl.program_id(2)
    acc_ref[...] = jnp.zeros_like(acc_ref)

    def step(t, carry, masked):
        start = pl.multiple_of(t * TK, TK)
        out = []
        for j in range(HB):
            m_prev, l_prev = carry[j]
            k = k_ref[0, j, pl.ds(start, TK), :]
            vt = vt_ref[0, j, :, pl.ds(start, TK)]
            st = _dot(k, qt_ref[0, j])
            if masked:
                kc = lax.broadcasted_iota(jnp.int32, (TK, TQ), 0) // CHUNK
                qc = lax.broadcasted_iota(jnp.int32, (TK, TQ), 1) // CHUNK
                st = jnp.where(kc <= qc, st, NEG_BIG)
            m_new = jnp.maximum(m_prev, jnp.max(st, axis=0, keepdims=True))
            alpha = jnp.exp2(m_prev - m_new)
            pt = jnp.exp2(st - m_new)
            l_new = alpha * l_prev + jnp.sum(pt, axis=0, keepdims=True)
            acc_ref[j] = alpha * acc_ref[j] + _dot(vt, pt.astype(jnp.bfloat16))
            out.append((m_new, l_new))
        return tuple(out)

    init = tuple((jnp.full((1, TQ), NEG_BIG, jnp.float32), jnp.zeros((1, TQ), jnp.float32))
                 for _ in range(HB))
    carry = lax.fori_loop(0, qi, lambda t, c: step(t, c, False), init)
    carry = step(qi, carry, True)
    for j in range(HB):
        ot = acc_ref[j] * (1.0 / carry[j][1])
        o_ref[:, j * DV:(j + 1) * DV] = ot.T.astype(o_ref.dtype)


def _attn_call(qt, k, vt, batch, seq):
    assert TQ == TK
    return pl.pallas_call(
        _attn_kernel,
        grid=(batch, HEADS // HB, seq // TQ),
        in_specs=[
            pl.BlockSpec((1, HB, DQK, TQ), lambda b, h, i: (b, h, 0, i)),
            pl.BlockSpec((1, HB, seq, DQK), lambda b, h, i: (b, h, 0, 0)),
            pl.BlockSpec((1, HB, DV, seq), lambda b, h, i: (b, h, 0, 0)),
        ],
        out_specs=pl.BlockSpec((TQ, HB * DV), lambda b, h, i: (b * (seq // TQ) + i, h)),
        out_shape=jax.ShapeDtypeStruct((batch * seq, MLA_W), jnp.bfloat16),
        scratch_shapes=[pltpu.VMEM((HB, DV, TQ), jnp.float32)],
        compiler_params=pltpu.CompilerParams(
            dimension_semantics=("parallel", "parallel", "arbitrary"),
            vmem_limit_bytes=VMEM_LIMIT),
        name="mla_attn",
    )(qt, k, vt)


def _out_kernel(x_ref, o_ref, gate_ref, og_ref, omg_ref, wo_ref, fg_ref, out_ref):
    om = _rms(o_ref[...].astype(jnp.float32), omg_ref[...]) * gate_ref[...].astype(jnp.float32)
    h = x_ref[...] + _dot(om.astype(jnp.bfloat16), wo_ref[0:MLA_W, :]) \
        + _dot(og_ref[...], wo_ref[MLA_W:, :])
    out_ref[...] = _rms(h, fg_ref[...])


def _out_call(x2, o_mla, gate, og, omg, wo, fg):
    n = x2.shape[0]
    tm = TM_OUT
    row_spec = pl.BlockSpec((tm, D_MODEL), lambda i: (i, 0))
    return pl.pallas_call(
        _out_kernel,
        grid=(n // tm,),
        in_specs=[row_spec, row_spec, row_spec, row_spec,
                  _const_spec(omg.shape), _const_spec(wo.shape), _const_spec(fg.shape)],
        out_specs=row_spec,
        out_shape=jax.ShapeDtypeStruct((n, D_MODEL), jnp.float32),
        compiler_params=pltpu.CompilerParams(
            dimension_semantics=("parallel",), vmem_limit_bytes=VMEM_LIMIT),
        name="out_proj",
    )(x2, o_mla, gate, og, omg, wo, fg)


def _rope_tables(seq):
    pos = jnp.arange(seq, dtype=jnp.float32)
    inv_freq = ROPE_THETA ** (-jnp.arange(0, ROPE, 2, dtype=jnp.float32) / ROPE)
    ang = pos[:, None] * inv_freq[None, :]
    cos, sin = jnp.cos(ang), jnp.sin(ang)
    ones = jnp.ones((seq, NOPE), jnp.float32)
    zeros_n = jnp.zeros((seq, NOPE), jnp.float32)
    pad = jnp.zeros((seq, DQK - NOPE - ROPE), jnp.float32)
    cos_t = jnp.concatenate([ones, cos, cos, pad], axis=1)
    sin_t = jnp.concatenate([zeros_n, sin, sin, pad], axis=1)
    return cos_t, sin_t


def _rot_half_cols(w):
    half = ROPE // 2
    return jnp.concatenate([-w[..., half:], w[..., :half]], axis=-1)


def _pack_weights(w_in, w_uq, w_ukv):
    bf = jnp.bfloat16
    d = w_in.shape[0]
    kr_w = w_in[:, 640:672]
    z64 = jnp.zeros((d, NOPE), w_in.dtype)
    z32 = jnp.zeros((d, DQK - NOPE - ROPE), w_in.dtype)
    win = jnp.concatenate([
        w_in[:, 0:640],
        z64, kr_w, z32,
        z64, _rot_half_cols(kr_w), z32,
        w_in[:, 672:],
    ], axis=1).astype(bf)
    wq = w_uq.reshape(Q_RANK, HEADS, NOPE + ROPE)
    zq = jnp.zeros((Q_RANK, HEADS, DQK - NOPE - ROPE), w_uq.dtype)
    wuq = jnp.concatenate([wq, zq], axis=-1).reshape(Q_RANK, HEADS * DQK).astype(bf)
    wuqr = jnp.concatenate([jnp.zeros((Q_RANK, HEADS, NOPE), w_uq.dtype),
                            _rot_half_cols(wq[..., NOPE:]), zq],
                           axis=-1).reshape(Q_RANK, HEADS * DQK).astype(bf)
    wkv = w_ukv.reshape(KV_RANK, HEADS, NOPE + DV)
    wkn = jnp.concatenate([wkv[..., :NOPE], jnp.zeros((KV_RANK, HEADS, DQK - NOPE), w_ukv.dtype)],
                          axis=-1).reshape(KV_RANK, HEADS * DQK).astype(bf)
    wv = wkv[..., NOPE:].reshape(KV_RANK, HEADS * DV).astype(bf)
    return win, wuq, wuqr, wkn, wv


def kernel(x, norm_in_g, w_in, q_norm_g, w_uq, kv_norm_g, w_ukv, gmlp_ln_g, gmlp_ln_b,
           w_spatial, b_spatial, out_norm_mla_g, out_norm_gmlp_g, w_out, final_norm_g):
    batch, seq, d = x.shape
    n = batch * seq
    x2 = x.reshape(n, d)
    row = lambda a: a.reshape(1, -1).astype(jnp.float32)

    win, wuq, wuqr, wkn, wv = _pack_weights(w_in, w_uq, w_ukv)
    cos_t, sin_t = _rope_tables(seq)
    ws = w_spatial.astype(jnp.float32)
    bs =jnp.broadcast_to(b_spatial[:, :, None], (GROUPS, SBLOCK, GDIM)).astype(jnp.float32)

    q, k, v, gate, og = _proj_call(
        x2, cos_t, sin_t, row(norm_in_g), win, row(q_norm_g), wuq, wuqr, row(kv_norm_g),
        wkn, wv, row(gmlp_ln_g), row(gmlp_ln_b), ws, bs, row(out_norm_gmlp_g), batch, seq)
    o_mla = _attn_call(q, k, v, batch, seq)
    out = _out_call(x2, o_mla, gate, og, row(out_norm_mla_g), w_out.astype(jnp.bfloat16),
                    row(final_norm_g))
    return out.reshape(batch, seq, d)
```

```python
import functools
import math

import jax
import jax.numpy as jnp
from jax import lax
from jax.experimental import pallas as pl
from jax.experimental.pallas import tpu as pltpu

D_MODEL = 1024
CHUNK = 64
EPS = 1e-6

HEADS = 8
NOPE = 64
ROPE = 32
DV = 128
DQK = 128
Q_RANK = 384
KV_RANK = 256
MLA_W = HEADS * DV
ROPE_THETA = 10000.0

GROUPS = 8
GDIM = 128
GMLP_W = GROUPS * GDIM
SBLOCK = 128

C_QLAT = 0
C_KVLAT = C_QLAT + Q_RANK
C_KR = C_KVLAT + KV_RANK
C_KRROT = C_KR + DQK
C_GMLA = C_KRROT + DQK
C_U = C_GMLA + MLA_W
C_V = C_U + GMLP_W
C_GG = C_V + GMLP_W
D_IN_PACKED = C_GG + GMLP_W

V7X_VMEM_BYTES = 64 * 1024 * 1024
VMEM_LIMIT = 56 * 1024 * 1024

TM_PROJ = 256
TQ = 512
TK = 512
HB = 4
TM_OUT = 512

NEG_BIG = -1e30


def _rms(x, g):
    return x * lax.rsqrt(jnp.mean(x * x, axis=-1, keepdims=True) + EPS) * g


def _dot(a, b):
    return jnp.dot(a, b, preferred_element_type=jnp.float32)


def _silu(x):
    return x * (1.0 / (1.0 + jnp.exp(-x)))


def _gelu(x):
    return 0.5 * x * (1.0 + lax.erf(x * (1.0 / math.sqrt(2.0))))


def _proj_kernel(x_ref, cos_ref, sin_ref, ng_ref, win_ref, qg_ref, wuq_ref, wuqr_ref,
                 kvg_ref, wkn_ref, wv_ref, lng_ref, lnb_ref, ws_ref, bs_ref, ogg_ref,
                 qt_ref, k_ref, vt_ref, gate_ref, og_ref):
    x = x_ref[...]
    y = _rms(x, ng_ref[...]).astype(jnp.bfloat16)

    cos = cos_ref[...]
    sin = sin_ref[...]
    scale = math.log2(math.e) / math.sqrt(NOPE + ROPE)

    lat = _dot(y, win_ref[:, C_QLAT:C_GMLA])
    qn = _rms(lat[:, C_QLAT:C_KVLAT], qg_ref[...]).astype(jnp.bfloat16)
    kvn = _rms(lat[:, C_KVLAT:C_KR], kvg_ref[...]).astype(jnp.bfloat16)
    kr = lat[:, C_KR:C_KRROT] * cos + lat[:, C_KRROT:C_GMLA] * sin

    q_all = _dot(qn, wuq_ref[...])
    q_rot = _dot(qn, wuqr_ref[...])
    k_all = _dot(kvn, wkn_ref[...])
    v_all = _dot(kvn, wv_ref[...])
    for h in range(HEADS):
        sl = slice(h * DQK, (h + 1) * DQK)
        qh = (q_all[:, sl] * cos + q_rot[:, sl] * sin) * scale
        qt_ref[0, h] = qh.T.astype(jnp.bfloat16)
        k_ref[0, h] = (k_all[:, sl] + kr).astype(jnp.bfloat16)
        vt_ref[0, h] = v_all[:, sl].T.astype(jnp.bfloat16)

    gate_ref[...] = _silu(_dot(y, win_ref[:, C_GMLA:C_U])).astype(jnp.bfloat16)

    u = _gelu(_dot(y, win_ref[:, C_U:C_V]))
    v = _gelu(_dot(y, win_ref[:, C_V:C_GG]))
    mu = jnp.mean(v, axis=-1, keepdims=True)
    vc = v - mu
    var = jnp.mean(vc * vc, axis=-1, keepdims=True)
    vln = (vc * lax.rsqrt(var + EPS) * lng_ref[...] + lnb_ref[...]).astype(jnp.bfloat16)

    tc = lax.broadcasted_iota(jnp.int32, (SBLOCK, SBLOCK), 0) // CHUNK
    sc = lax.broadcasted_iota(jnp.int32, (SBLOCK, SBLOCK), 1) // CHUNK
    ws = [jnp.where(sc <= tc, ws_ref[g], 0.0).astype(jnp.bfloat16) for g in range(GROUPS)]

    tm = x.shape[0]
    rows = []
    for c in range(tm // SBLOCK):
        cols = []
        for g in range(GROUPS):
            vb = vln[c * SBLOCK:(c + 1) * SBLOCK, g * GDIM:(g + 1) * GDIM]
            cols.append(_dot(ws[g], vb) + bs_ref[g])
        rows.append(jnp.concatenate(cols, axis=1))
    mixed = jnp.concatenate(rows, axis=0)
    og = _rms(u * mixed, ogg_ref[...])
    og_ref[...] = (og * _silu(_dot(y, win_ref[:, C_GG:D_IN_PACKED]))).astype(jnp.bfloat16)


def _const_spec(shape):
    nd = len(shape)
    return pl.BlockSpec(shape, lambda i, _nd=nd: (0,) * _nd)


def _proj_call(x2, cos_t, sin_t, ng, win, qg, wuq, wuqr, kvg, wkn, wv, lng, lnb, ws, bs, ogg,
               batch, seq):
    n = x2.shape[0]
    tm = TM_PROJ
    tiles_per_seq = seq // tm
    row_spec = pl.BlockSpec((tm, D_MODEL), lambda i: (i, 0))
    tab_spec = pl.BlockSpec((tm, DQK), lambda i: (i % tiles_per_seq, 0))
    head_spec = pl.BlockSpec((1, HEADS, tm, DQK),
                             lambda i: (i // tiles_per_seq, 0, i % tiles_per_seq, 0))
    head_shape = jax.ShapeDtypeStruct((batch, HEADS, seq, DQK), jnp.bfloat16)
    headt_spec = pl.BlockSpec((1, HEADS, DQK, tm),
                              lambda i: (i // tiles_per_seq, 0, 0, i % tiles_per_seq))
    headt_shape = jax.ShapeDtypeStruct((batch, HEADS, DQK, seq), jnp.bfloat16)
    row_shape = jax.ShapeDtypeStruct((n, D_MODEL), jnp.bfloat16)
    consts = (ng, win, qg, wuq, wuqr, kvg, wkn, wv, lng, lnb, ws, bs, ogg)
    return pl.pallas_call(
        _proj_kernel,
        grid=(n // tm,),
        in_specs=[row_spec, tab_spec, tab_spec] + [_const_spec(c.shape) for c in consts],
        out_specs=[headt_spec, head_spec, headt_spec, row_spec, row_spec],
        out_shape=[headt_shape, head_shape, headt_shape, row_shape, row_shape],
        compiler_params=pltpu.CompilerParams(
            dimension_semantics=("parallel",), vmem_limit_bytes=VMEM_LIMIT),
        name="proj_gmlp",
    )(x2, cos_t, sin_t, *consts)


def _attn_kernel(qt_ref, k_ref, vt_ref, o_ref, s_ref, acc_ref):
    qi = pl.program_id(2)
    acc_ref[...] = jnp.zeros_like(acc_ref)

    def scores(t, j):
        start = pl.multiple_of(t * TK, TK)
        return _dot(k_ref[0, j, pl.ds(start, TK), :], qt_ref[0, j])

    def consume(t, j, st, state, cmax):
        m_prev, l_prev = state
        start = pl.multiple_of(t * TK, TK)
        m_new = jnp.maximum(m_prev, cmax)
        alpha = jnp.exp2(m_prev - m_new)
        pt = jnp.exp2(st - m_new)
        l_new = alpha * l_prev + jnp.sum(pt, axis=0, keepdims=True)
        vt = vt_ref[0, j, :, pl.ds(start, TK)]
        return m_new, l_new, alpha, _dot(vt, pt.astype(jnp.bfloat16))

    init = []
    for j in range(HB):
        st = scores(0, j)
        s_ref[j] = st
        init.append((jnp.full((1, TQ), NEG_BIG, jnp.float32), jnp.zeros((1, TQ), jnp.float32),
                     jnp.max(st, axis=0, keepdims=True)))

    def body(t, carry):
        out = []
        for j in range(HB):
            m_prev, l_prev, cmax = carry[j]
            m_new, l_new, alpha, pv = consume(t, j, s_ref[j], (m_prev, l_prev), cmax)
            sn = scores(t + 1, j)
            s_ref[j] = sn
            acc_ref[j] = alpha * acc_ref[j] + pv
            out.append((m_new, l_new, jnp.max(sn, axis=0, keepdims=True)))
        return tuple(out)

    carry = lax.fori_loop(0, qi, body, tuple(init))

    kc = lax.broadcasted_iota(jnp.int32, (TK, TQ), 0) // CHUNK
    qc = lax.broadcasted_iota(jnp.int32, (TK, TQ), 1) // CHUNK
    for j in range(HB):
        m_prev, l_prev, _ = carry[j]
        st = jnp.where(kc <= qc, s_ref[j], NEG_BIG)
        _, l_new, alpha, pv = consume(qi, j, st, (m_prev, l_prev),
                                      jnp.max(st, axis=0, keepdims=True))
        ot = (alpha * acc_ref[j] + pv) * (1.0 / l_new)
        o_ref[:, j * DV:(j + 1) * DV] = ot.T.astype(o_ref.dtype)


def _attn_call(qt, k, vt, batch, seq):
    assert TQ == TK
    return pl.pallas_call(
        _attn_kernel,
        grid=(batch, HEADS // HB, seq // TQ),
        in_specs=[
            pl.BlockSpec((1, HB, DQK, TQ), lambda b, h, i: (b, h, 0, i)),
            pl.BlockSpec((1, HB, seq, DQK), lambda b, h, i: (b, h, 0, 0)),
            pl.BlockSpec((1, HB, DV, seq), lambda b, h, i: (b, h, 0, 0)),
        ],
        out_specs=pl.BlockSpec((TQ, HB * DV), lambda b, h, i: (b * (seq // TQ) + i, h)),
        out_shape=jax.ShapeDtypeStruct((batch * seq, MLA_W), jnp.bfloat16),
        scratch_shapes=[pltpu.VMEM((HB, TK, TQ), jnp.float32),
                        pltpu.VMEM((HB, DV, TQ), jnp.float32)],
        compiler_params=pltpu.CompilerParams(
            dimension_semantics=("parallel", "parallel", "arbitrary"),
            vmem_limit_bytes=VMEM_LIMIT),
        name="mla_attn",
    )(qt, k, vt)


def _out_kernel(x_ref, o_ref, gate_ref, og_ref, omg_ref, wo_ref, fg_ref, out_ref):
    om = _rms(o_ref[...].astype(jnp.float32), omg_ref[...]) * gate_ref[...].astype(jnp.float32)
    h = x_ref[...] + _dot(om.astype(jnp.bfloat16), wo_ref[0:MLA_W, :]) \
        + _dot(og_ref[...], wo_ref[MLA_W:, :])
    out_ref[...] = _rms(h, fg_ref[...])


def _out_call(x2, o_mla, gate, og, omg, wo, fg):
    n = x2.shape[0]
    tm = TM_OUT
    row_spec = pl.BlockSpec((tm, D_MODEL), lambda i: (i, 0))
    return pl.pallas_call(
        _out_kernel,
        grid=(n // tm,),
        in_specs=[row_spec, row_spec, row_spec, row_spec,
                  _const_spec(omg.shape), _const_spec(wo.shape), _const_spec(fg.shape)],
        out_specs=row_spec,
        out_shape=jax.ShapeDtypeStruct((n, D_MODEL), jnp.float32),
        compiler_params=pltpu.CompilerParams(
            dimension_semantics=("parallel",), vmem_limit_bytes=VMEM_LIMIT),
        name="out_proj",
    )(x2, o_mla, gate, og, omg, wo, fg)


def _rope_tables(seq):
    pos = jnp.arange(seq, dtype=jnp.float32)
    inv_freq = ROPE_THETA ** (-jnp.arange(0, ROPE, 2, dtype=jnp.float32) / ROPE)
    ang = pos[:, None] * inv_freq[None, :]
    cos, sin = jnp.cos(ang), jnp.sin(ang)
    ones = jnp.ones((seq, NOPE), jnp.float32)
    zeros_n = jnp.zeros((seq, NOPE), jnp.float32)
    pad = jnp.zeros((seq, DQK - NOPE - ROPE), jnp.float32)
    cos_t = jnp.concatenate([ones, cos, cos, pad], axis=1)
    sin_t = jnp.concatenate([zeros_n, sin, sin, pad], axis=1)
    return cos_t, sin_t


def _rot_half_cols(w):
    half = ROPE // 2
    return jnp.concatenate([-w[..., half:], w[..., :half]], axis=-1)


def _pack_weights(w_in, w_uq, w_ukv):
    bf = jnp.bfloat16
    d = w_in.shape[0]
    kr_w = w_in[:, 640:672]
    z64 = jnp.zeros((d, NOPE), w_in.dtype)
    z32 = jnp.zeros((d, DQK - NOPE - ROPE), w_in.dtype)
    win = jnp.concatenate([
        w_in[:, 0:640],
        z64, kr_w, z32,
        z64, _rot_half_cols(kr_w), z32,
        w_in[:, 672:],
    ], axis=1).astype(bf)
    wq = w_uq.reshape(Q_RANK, HEADS, NOPE + ROPE)
    zq = jnp.zeros((Q_RANK, HEADS, DQK - NOPE - ROPE), w_uq.dtype)
    wuq = jnp.concatenate([wq, zq], axis=-1).reshape(Q_RANK, HEADS * DQK).astype(bf)
    wuqr = jnp.concatenate([jnp.zeros((Q_RANK, HEADS, NOPE), w_uq.dtype),
                            _rot_half_cols(wq[..., NOPE:]), zq],
                           axis=-1).reshape(Q_RANK, HEADS * DQK).astype(bf)
    wkv = w_ukv.reshape(KV_RANK, HEADS, NOPE + DV)
    wkn = jnp.concatenate([wkv[..., :NOPE], jnp.zeros((KV_RANK, HEADS, DQK - NOPE), w_ukv.dtype)],
                          axis=-1).reshape(KV_RANK, HEADS * DQK).astype(bf)
    wv = wkv[..., NOPE:].reshape(KV_RANK, HEADS * DV).astype(bf)
    return win, wuq, wuqr, wkn, wv


def kernel(x, norm_in_g, w_in, q_norm_g, w_uq, kv_norm_g, w_ukv, gmlp_ln_g, gmlp_ln_b,
           w_spatial, b_spatial, out_norm_mla_g, out_norm_gmlp_g, w_out, final_norm_g):
    batch, seq, d = x.shape
    n = batch * seq
    x2 = x.reshape(n, d)
    row = lambda a: a.reshape(1, -1).astype(jnp.float32)

    win, wuq, wuqr, wkn, wv = _pack_weights(w_in, w_uq, w_ukv)
    cos_t, sin_t = _rope_tables(seq)
    ws = w_spatial.astype(jnp.float32)
    bs =jnp.broadcast_to(b_spatial[:, :, None], (GROUPS, SBLOCK, GDIM)).astype(jnp.float32)

    q, k, v, gate, og = _proj_call(
        x2, cos_t, sin_t, row(norm_in_g), win, row(q_norm_g), wuq, wuqr, row(kv_norm_g),
        wkn, wv, row(gmlp_ln_g), row(gmlp_ln_b), ws, bs, row(out_norm_gmlp_g), batch, seq)
    o_mla = _attn_call(q, k, v, batch, seq)
    out = _out_call(x2, o_mla, gate, og, row(out_norm_mla_g), w_out.astype(jnp.bfloat16),
                    row(final_norm_g))
    return out.reshape(batch, seq, d)
```

```python
import functools
import math

import jax
import jax.numpy as jnp
from jax import lax
from jax.experimental import pallas as pl
from jax.experimental.pallas import tpu as pltpu

D_MODEL = 1024
CHUNK = 64
EPS = 1e-6

HEADS = 8
NOPE = 64
ROPE = 32
DV = 128
DQK = 128
Q_RANK = 384
KV_RANK = 256
MLA_W = HEADS * DV
ROPE_THETA = 10000.0

GROUPS = 8
GDIM = 128
GMLP_W = GROUPS * GDIM
SBLOCK = 128

C_QLAT = 0
C_KVLAT = C_QLAT + Q_RANK
C_KR = C_KVLAT + KV_RANK
C_KRROT = C_KR + DQK
C_GMLA = C_KRROT + DQK
C_U = C_GMLA + MLA_W
C_V = C_U + GMLP_W
C_GG = C_V + GMLP_W
D_IN_PACKED = C_GG + GMLP_W

V7X_VMEM_BYTES = 64 * 1024 * 1024
VMEM_LIMIT = 56 * 1024 * 1024

TM_PROJ = 256
TQ = 512
TK = 512
HB = 8
LB = 128
TM_OUT = 512

NEG_BIG = -1e30


def _rms(x, g):
    return x * lax.rsqrt(jnp.mean(x * x, axis=-1, keepdims=True) + EPS) * g


def _dot(a, b):
    return jnp.dot(a, b, preferred_element_type=jnp.float32)


def _silu(x):
    return x * (1.0 / (1.0 + jnp.exp(-x)))


def _gelu(x):
    return 0.5 * x * (1.0 + lax.erf(x * (1.0 / math.sqrt(2.0))))


def _proj_kernel(x_ref, cos_ref, sin_ref, ng_ref, win_ref, qg_ref, wuq_ref, wuqr_ref,
                 kvg_ref, wkn_ref, wv_ref, lng_ref, lnb_ref, ws_ref, bs_ref, ogg_ref,
                 qt_ref, k_ref, vt_ref, gate_ref, og_ref):
    x = x_ref[...]
    y = _rms(x, ng_ref[...]).astype(jnp.bfloat16)

    cos = cos_ref[...]
    sin = sin_ref[...]
    scale = math.log2(math.e) / math.sqrt(NOPE + ROPE)

    lat = _dot(y, win_ref[:, C_QLAT:C_GMLA])
    qn = _rms(lat[:, C_QLAT:C_KVLAT], qg_ref[...]).astype(jnp.bfloat16)
    kvn = _rms(lat[:, C_KVLAT:C_KR], kvg_ref[...]).astype(jnp.bfloat16)
    kr = lat[:, C_KR:C_KRROT] * cos + lat[:, C_KRROT:C_GMLA] * sin

    q_all = _dot(qn, wuq_ref[...])
    q_rot = _dot(qn, wuqr_ref[...])
    k_all = _dot(kvn, wkn_ref[...])
    v_all = _dot(kvn, wv_ref[...])
    for h in range(HEADS):
        sl = slice(h * DQK, (h + 1) * DQK)
        qh = (q_all[:, sl] * cos + q_rot[:, sl] * sin) * scale
        qt_ref[0, h] = qh.T.astype(jnp.bfloat16)
        k_ref[0, h] = (k_all[:, sl] + kr).astype(jnp.bfloat16)
        vt_ref[0, h] = v_all[:, sl].T.astype(jnp.bfloat16)

    gate_ref[...] = _silu(_dot(y, win_ref[:, C_GMLA:C_U])).astype(jnp.bfloat16)

    u = _gelu(_dot(y, win_ref[:, C_U:C_V]))
    v = _gelu(_dot(y, win_ref[:, C_V:C_GG]))
    mu = jnp.mean(v, axis=-1, keepdims=True)
    vc = v - mu
    var = jnp.mean(vc * vc, axis=-1, keepdims=True)
    vln = (vc * lax.rsqrt(var + EPS) * lng_ref[...] + lnb_ref[...]).astype(jnp.bfloat16)

    tc = lax.broadcasted_iota(jnp.int32, (SBLOCK, SBLOCK), 0) // CHUNK
    sc = lax.broadcasted_iota(jnp.int32, (SBLOCK, SBLOCK), 1) // CHUNK
    ws = [jnp.where(sc <= tc, ws_ref[g], 0.0).astype(jnp.bfloat16) for g in range(GROUPS)]

    tm = x.shape[0]
    rows = []
    for c in range(tm // SBLOCK):
        cols = []
        for g in range(GROUPS):
            vb = vln[c * SBLOCK:(c + 1) * SBLOCK, g * GDIM:(g + 1) * GDIM]
            cols.append(_dot(ws[g], vb) + bs_ref[g])
        rows.append(jnp.concatenate(cols, axis=1))
    mixed = jnp.concatenate(rows, axis=0)
    og = _rms(u * mixed, ogg_ref[...])
    og_ref[...] = (og * _silu(_dot(y, win_ref[:, C_GG:D_IN_PACKED]))).astype(jnp.bfloat16)


def _const_spec(shape):
    nd = len(shape)
    return pl.BlockSpec(shape, lambda i, _nd=nd: (0,) * _nd)


def _proj_call(x2, cos_t, sin_t, ng, win, qg, wuq, wuqr, kvg, wkn, wv, lng, lnb, ws, bs, ogg,
               batch, seq):
    n = x2.shape[0]
    tm = TM_PROJ
    tiles_per_seq = seq // tm
    row_spec = pl.BlockSpec((tm, D_MODEL), lambda i: (i, 0))
    tab_spec = pl.BlockSpec((tm, DQK), lambda i: (i % tiles_per_seq, 0))
    head_spec = pl.BlockSpec((1, HEADS, tm, DQK),
                             lambda i: (i // tiles_per_seq, 0, i % tiles_per_seq, 0))
    head_shape = jax.ShapeDtypeStruct((batch, HEADS, seq, DQK), jnp.bfloat16)
    headt_spec = pl.BlockSpec((1, HEADS, DQK, tm),
                              lambda i: (i // tiles_per_seq, 0, 0, i % tiles_per_seq))
    headt_shape = jax.ShapeDtypeStruct((batch, HEADS, DQK, seq), jnp.bfloat16)
    row_shape = jax.ShapeDtypeStruct((n, D_MODEL), jnp.bfloat16)
    consts = (ng, win, qg, wuq, wuqr, kvg, wkn, wv, lng, lnb, ws, bs, ogg)
    return pl.pallas_call(
        _proj_kernel,
        grid=(n // tm,),
        in_specs=[row_spec, tab_spec, tab_spec] + [_const_spec(c.shape) for c in consts],
        out_specs=[headt_spec, head_spec, headt_spec, row_spec, row_spec],
        out_shape=[headt_shape, head_shape, headt_shape, row_shape, row_shape],
        compiler_params=pltpu.CompilerParams(
            dimension_semantics=("parallel",), vmem_limit_bytes=VMEM_LIMIT),
        name="proj_gmlp",
    )(x2, cos_t, sin_t, *consts)


def _attn_kernel(qt_ref, qtn_ref, k_ref, vt_ref, o_ref, s_ref, cm_ref, acc_ref):
    qi = pl.program_id(2)
    acc_ref[...] = jnp.zeros_like(acc_ref)

    def scores(q_ref, t, j):
        start = pl.multiple_of(t * TK, TK)
        return _dot(k_ref[0, j, pl.ds(start, TK), :], q_ref[0, j])

    def park(j, st):
        s_ref[j] = st
        cm_ref[j] = jnp.max(st, axis=0, keepdims=True)

    def consume(t, j, st, state, cmax):
        m_prev, l_prev = state
        start = pl.multiple_of(t * TK, TK)
        m_new = jnp.maximum(m_prev, cmax)
        alpha = jnp.exp2(m_prev - m_new)
        pt = jnp.exp2(st - m_new)
        l_new = alpha * l_prev + jnp.sum(pt, axis=0, keepdims=True)
        vt = vt_ref[0, j, :, pl.ds(start, TK)]
        return m_new, l_new, alpha, _dot(vt, pt.astype(jnp.bfloat16))

    @pl.when(qi == 0)
    def _():
        for j in range(HB):
            park(j, scores(qt_ref, 0, j))

    def body(t, carry):
        out = []
        for j in range(HB):
            m_prev, l_prev = carry[j]
            m_new, l_new, alpha, pv = consume(t, j, s_ref[j], (m_prev, l_prev), cm_ref[j])
            park(j, scores(qt_ref, t + 1, j))
            acc_ref[j] = alpha * acc_ref[j] + pv
            out.append((m_new, l_new))
        return tuple(out)

    init = tuple((jnp.full((1, TQ), NEG_BIG, jnp.float32), jnp.zeros((1, TQ), jnp.float32))
                 for _ in range(HB))
    carry = lax.fori_loop(0, qi, body, init)

    kc = lax.broadcasted_iota(jnp.int32, (LB, LB), 0) // CHUNK
    qc = lax.broadcasted_iota(jnp.int32, (LB, LB), 1) // CHUNK
    diag_ok = kc <= qc
    start = pl.multiple_of(qi * TK, TK)
    for j in range(HB):
        m_prev, l_prev = carry[j]
        p_blocks, m_blocks, l_blocks = [], [], []
        for c in range(TQ // LB):
            lanes = slice(c * LB, (c + 1) * LB)
            sq = jnp.where(diag_ok, s_ref[j, c * LB:(c + 1) * LB, lanes], NEG_BIG)
            if c:
                st = jnp.concatenate([s_ref[j, 0:c * LB, lanes], sq], axis=0)
            else:
                st = sq
            m_new = jnp.maximum(m_prev[:, lanes], jnp.max(st, axis=0, keepdims=True))
            pt = jnp.exp2(st - m_new)
            m_blocks.append(m_new)
            l_blocks.append(jnp.sum(pt, axis=0, keepdims=True))
            pt = pt.astype(jnp.bfloat16)
            if (c + 1) * LB < TK:
                pt = jnp.concatenate(
                    [pt, jnp.zeros((TK - (c + 1) * LB, LB), jnp.bfloat16)], axis=0)
            p_blocks.append(pt)
        m_new = jnp.concatenate(m_blocks, axis=1)
        alpha = jnp.exp2(m_prev - m_new)
        l_new = alpha * l_prev + jnp.concatenate(l_blocks, axis=1)
        pv = _dot(vt_ref[0, j, :, pl.ds(start, TK)], jnp.concatenate(p_blocks, axis=1))
        park(j, scores(qtn_ref, 0, j))
        ot = (alpha * acc_ref[j] + pv) * (1.0 / l_new)
        o_ref[:, j * DV:(j + 1) * DV] = ot.T.astype(o_ref.dtype)


def _attn_call(qt, k, vt, batch, seq):
    assert TQ == TK
    nq = seq // TQ
    return pl.pallas_call(
        _attn_kernel,
        grid=(batch, HEADS // HB, seq // TQ),
        in_specs=[
            pl.BlockSpec((1, HB, DQK, TQ), lambda b, h, i: (b, h, 0, i)),
            pl.BlockSpec((1, HB, DQK, TQ), lambda b, h, i: (b, h, 0, jnp.minimum(i + 1, nq - 1))),
            pl.BlockSpec((1, HB, seq, DQK), lambda b, h, i: (b, h, 0, 0),
                         pipeline_mode=pl.Buffered(1)),
            pl.BlockSpec((1, HB, DV, seq), lambda b, h, i: (b, h, 0, 0),
                         pipeline_mode=pl.Buffered(1)),
        ],
        out_specs=pl.BlockSpec((TQ, HB * DV), lambda b, h, i: (b * (seq // TQ) + i, h)),
        out_shape=jax.ShapeDtypeStruct((batch * seq, MLA_W), jnp.bfloat16),
        scratch_shapes=[pltpu.VMEM((HB, TK, TQ), jnp.float32),
                        pltpu.VMEM((HB, 1, TQ), jnp.float32),
                        pltpu.VMEM((HB, DV, TQ), jnp.float32)],
        compiler_params=pltpu.CompilerParams(
            dimension_semantics=("parallel", "parallel", "arbitrary"),
            vmem_limit_bytes=VMEM_LIMIT),
        name="mla_attn",
    )(qt, qt, k, vt)


def _out_kernel(x_ref, o_ref, gate_ref, og_ref, omg_ref, wo_ref, fg_ref, out_ref):
    om = _rms(o_ref[...].astype(jnp.float32), omg_ref[...]) * gate_ref[...].astype(jnp.float32)
    h = x_ref[...] + _dot(om.astype(jnp.bfloat16), wo_ref[0:MLA_W, :]) \
        + _dot(og_ref[...], wo_ref[MLA_W:, :])
    out_ref[...] = _rms(h, fg_ref[...])


def _out_call(x2, o_mla, gate, og, omg, wo, fg):
    n = x2.shape[0]
    tm = TM_OUT
    row_spec = pl.BlockSpec((tm, D_MODEL), lambda i: (i, 0))
    return pl.pallas_call(
        _out_kernel,
        grid=(n // tm,),
        in_specs=[row_spec, row_spec, row_spec, row_spec,
                  _const_spec(omg.shape), _const_spec(wo.shape), _const_spec(fg.shape)],
        out_specs=row_spec,
        out_shape=jax.ShapeDtypeStruct((n, D_MODEL), jnp.float32),
        compiler_params=pltpu.CompilerParams(
            dimension_semantics=("parallel",), vmem_limit_bytes=VMEM_LIMIT),
        name="out_proj",
    )(x2, o_mla, gate, og, omg, wo, fg)


def _rope_tables(seq):
    pos = jnp.arange(seq, dtype=jnp.float32)
    inv_freq = ROPE_THETA ** (-jnp.arange(0, ROPE, 2, dtype=jnp.float32) / ROPE)
    ang = pos[:, None] * inv_freq[None, :]
    cos, sin = jnp.cos(ang), jnp.sin(ang)
    ones = jnp.ones((seq, NOPE), jnp.float32)
    zeros_n = jnp.zeros((seq, NOPE), jnp.float32)
    pad = jnp.zeros((seq, DQK - NOPE - ROPE), jnp.float32)
    cos_t = jnp.concatenate([ones, cos, cos, pad], axis=1)
    sin_t = jnp.concatenate([zeros_n, sin, sin, pad], axis=1)
    return cos_t, sin_t


def _rot_half_cols(w):
    half = ROPE // 2
    return jnp.concatenate([-w[..., half:], w[..., :half]], axis=-1)


def _pack_weights(w_in, w_uq, w_ukv):
    bf = jnp.bfloat16
    d = w_in.shape[0]
    kr_w = w_in[:, 640:672]
    z64 = jnp.zeros((d, NOPE), w_in.dtype)
    z32 = jnp.zeros((d, DQK - NOPE - ROPE), w_in.dtype)
    win = jnp.concatenate([
        w_in[:, 0:640],
        z64, kr_w, z32,
        z64, _rot_half_cols(kr_w), z32,
        w_in[:, 672:],
    ], axis=1).astype(bf)
    wq = w_uq.reshape(Q_RANK, HEADS, NOPE + ROPE)
    zq = jnp.zeros((Q_RANK, HEADS, DQK - NOPE - ROPE), w_uq.dtype)
    wuq = jnp.concatenate([wq, zq], axis=-1).reshape(Q_RANK, HEADS * DQK).astype(bf)
    wuqr = jnp.concatenate([jnp.zeros((Q_RANK, HEADS, NOPE), w_uq.dtype),
                            _rot_half_cols(wq[..., NOPE:]), zq],
                           axis=-1).reshape(Q_RANK, HEADS * DQK).astype(bf)
    wkv = w_ukv.reshape(KV_RANK, HEADS, NOPE + DV)
    wkn = jnp.concatenate([wkv[..., :NOPE], jnp.zeros((KV_RANK, HEADS, DQK - NOPE), w_ukv.dtype)],
                          axis=-1).reshape(KV_RANK, HEADS * DQK).astype(bf)
    wv = wkv[..., NOPE:].reshape(KV_RANK, HEADS * DV).astype(bf)
    return win, wuq, wuqr, wkn, wv


def kernel(x, norm_in_g, w_in, q_norm_g, w_uq, kv_norm_g, w_ukv, gmlp_ln_g, gmlp_ln_b,
           w_spatial, b_spatial, out_norm_mla_g, out_norm_gmlp_g, w_out, final_norm_g):
    batch, seq, d = x.shape
    n = batch * seq
    x2 = x.reshape(n, d)
    row = lambda a: a.reshape(1, -1).astype(jnp.float32)

    win, wuq, wuqr, wkn, wv = _pack_weights(w_in, w_uq, w_ukv)
    cos_t, sin_t = _rope_tables(seq)
    ws = w_spatial.astype(jnp.float32)
    bs =jnp.broadcast_to(b_spatial[:, :, None], (GROUPS, SBLOCK, GDIM)).astype(jnp.float32)

    q, k, v, gate, og = _proj_call(
        x2, cos_t, sin_t, row(norm_in_g), win, row(q_norm_g), wuq, wuqr, row(kv_norm_g),
        wkn, wv, row(gmlp_ln_g), row(gmlp_ln_b), ws, bs, row(out_norm_gmlp_g), batch, seq)
    o_mla = _attn_call(q, k, v, batch, seq)
    out = _out_call(x2, o_mla, gate, og, row(out_norm_mla_g), w_out.astype(jnp.bfloat16),
                    row(final_norm_g))
    return out.reshape(batch, seq, d)
```

```python
import math

import jax
import jax.numpy as jnp
from jax import lax
from jax.experimental import pallas as pl
from jax.experimental.pallas import tpu as pltpu

D_MODEL = 1024
CHUNK = 64
EPS = 1e-6

HEADS = 8
NOPE = 64
ROPE = 32
DV = 128
DQK = 128
Q_RANK = 384
KV_RANK = 256
MLA_W = HEADS * DV
ROPE_THETA = 10000.0

GROUPS = 8
GDIM = 128
GMLP_W = GROUPS * GDIM
SBLOCK = 128

L_QLAT = 0
L_KVLAT = L_QLAT + Q_RANK
L_KR = L_KVLAT + KV_RANK
D_LAT = L_KR + DQK
B_GMLA = 0
B_U = B_GMLA + MLA_W
B_V = B_U + GMLP_W
B_GG = B_V + GMLP_W
D_BIG = B_GG + GMLP_W

V7X_LANES = 128
VMEM_LIMIT = 56 * 1024 * 1024

TM_PROJ = 512
SUB_PROJ = 256
TQ = 512
TK = 512
HB = 8
LB = 128
TM_OUT = 512
SUB_OUT = 256

NEG_BIG = -1e30


def _rms(x, g):
    return x * lax.rsqrt(jnp.mean(x * x, axis=-1, keepdims=True) + EPS) * g


def _dot(a, b):
    return jnp.dot(a, b, preferred_element_type=jnp.float32)


def _silu(x):
    return x * (1.0 / (1.0 + jnp.exp(-x)))


def _gelu(x):
    return 0.5 * x * (1.0 + lax.erf(x * (1.0 / math.sqrt(2.0))))


def _rope(x, cos, sin):
    return x * cos + pltpu.roll(x, DQK - ROPE, 1) * sin


def _proj_kernel(x_ref, cosq_ref, sinq_ref, cosk_ref, sink_ref, ng_ref, wlat_ref, wbig_ref,
                 qg_ref, wuq_ref, kvg_ref, wkn_ref, wv_ref, lng_ref, lnb_ref, ws_ref, bs_ref,
                 ogg_ref, qt_ref, k_ref, vt_ref, gate_ref, og_ref):
    tc = lax.broadcasted_iota(jnp.int32, (SBLOCK, SBLOCK), 0) // CHUNK
    sc = lax.broadcasted_iota(jnp.int32, (SBLOCK, SBLOCK), 1) // CHUNK
    ws = [jnp.where(sc <= tc, ws_ref[g], 0.0).astype(jnp.bfloat16) for g in range(GROUPS)]
    nblk = SUB_PROJ // SBLOCK

    for r0 in range(0, TM_PROJ, SUB_PROJ):
        rows = slice(r0, r0 + SUB_PROJ)
        y = _rms(x_ref[rows, :], ng_ref[...]).astype(jnp.bfloat16)

        v = _gelu(_dot(y, wbig_ref[:, B_V:B_GG]))
        u = _gelu(_dot(y, wbig_ref[:, B_U:B_V]))
        mu = jnp.mean(v, axis=-1, keepdims=True)
        vc = v - mu
        var = jnp.mean(vc * vc, axis=-1, keepdims=True)
        vln = (vc * lax.rsqrt(var + EPS) * lng_ref[...] + lnb_ref[...]).astype(jnp.bfloat16)

        lat = _dot(y, wlat_ref[...])
        gate_ref[rows, :] = _silu(_dot(y, wbig_ref[:, B_GMLA:B_U])).astype(jnp.bfloat16)

        mixed_g = []
        for g in range(GROUPS):
            cols = slice(g * GDIM, (g + 1) * GDIM)
            vb = jnp.concatenate([vln[c * SBLOCK:(c + 1) * SBLOCK, cols] for c in range(nblk)],
                                 axis=1)
            mixed_g.append(_dot(ws[g], vb))
        mixed = jnp.concatenate(
            [jnp.concatenate([mixed_g[g][:, c * GDIM:(c + 1) * GDIM] + bs_ref[g]
                              for g in range(GROUPS)], axis=1) for c in range(nblk)], axis=0)
        ggs = _silu(_dot(y, wbig_ref[:, B_GG:D_BIG]))
        og_ref[rows, :] = (_rms(u * mixed, ogg_ref[...]) * ggs).astype(jnp.bfloat16)

        qn = _rms(lat[:, L_QLAT:L_KVLAT], qg_ref[...]).astype(jnp.bfloat16)
        kvn = _rms(lat[:, L_KVLAT:L_KR], kvg_ref[...]).astype(jnp.bfloat16)
        kr = _rope(lat[:, L_KR:D_LAT], cosk_ref[rows, :], sink_ref[rows, :])
        q_all = _dot(qn, wuq_ref[...])
        k_all = _dot(kvn, wkn_ref[...])
        v_all = _dot(kvn, wv_ref[...])
        cosq = cosq_ref[rows, :]
        sinq = sinq_ref[rows, :]
        for h in range(HEADS):
            sl = slice(h * DQK, (h + 1) * DQK)
            qt_ref[0, h, :, rows] = _rope(q_all[:, sl], cosq, sinq).T.astype(jnp.bfloat16)
            k_ref[0, h, rows, :] = (k_all[:, sl] + kr).astype(jnp.bfloat16)
            vt_ref[0, h, :, rows] = v_all[:, sl].T.astype(jnp.bfloat16)


def _const_spec(shape):
    nd = len(shape)
    return pl.BlockSpec(shape, lambda i, _nd=nd: (0,) * _nd, pipeline_mode=pl.Buffered(1))


def _proj_call(x2, tables, consts, batch, seq):
    n = x2.shape[0]
    tm = TM_PROJ
    tiles_per_seq = seq // tm
    row_spec = pl.BlockSpec((tm, D_MODEL), lambda i: (i, 0))
    tab_spec = pl.BlockSpec((tm, DQK), lambda i: (i % tiles_per_seq, 0))
    head_spec = pl.BlockSpec((1, HEADS, tm, DQK),
                             lambda i: (i // tiles_per_seq, 0, i % tiles_per_seq, 0))
    head_shape = jax.ShapeDtypeStruct((batch, HEADS, seq, DQK), jnp.bfloat16)
    headt_spec = pl.BlockSpec((1, HEADS, DQK, tm),
                              lambda i: (i // tiles_per_seq, 0, 0, i % tiles_per_seq))
    headt_shape = jax.ShapeDtypeStruct((batch, HEADS, DQK, seq), jnp.bfloat16)
    row_shape = jax.ShapeDtypeStruct((n, D_MODEL), jnp.bfloat16)
    return pl.pallas_call(
        _proj_kernel,
        grid=(n // tm,),
        in_specs=[row_spec] + [tab_spec] * len(tables) + [_const_spec(c.shape) for c in consts],
        out_specs=[headt_spec, head_spec, headt_spec, row_spec, row_spec],
        out_shape=[headt_shape, head_shape, headt_shape, row_shape, row_shape],
        compiler_params=pltpu.CompilerParams(
            dimension_semantics=("parallel",), vmem_limit_bytes=VMEM_LIMIT),
        name="proj_gmlp",
    )(x2, *tables, *consts)


def _attn_kernel(qt_ref, qtn_ref, k_ref, vt_ref, o_ref, s_ref, cm_ref, acc_ref):
    qi = pl.program_id(2)
    acc_ref[...] = jnp.zeros_like(acc_ref)

    def scores(q_ref, t, j):
        start = pl.multiple_of(t * TK, TK)
        return _dot(k_ref[0, j, pl.ds(start, TK), :], q_ref[0, j])

    def park(j, st):
        s_ref[j] = st
        cm_ref[j] = jnp.max(st, axis=0, keepdims=True)

    def consume(t, j, st, state, cmax):
        m_prev, l_prev = state
        start = pl.multiple_of(t * TK, TK)
        m_new = jnp.maximum(m_prev, cmax)
        alpha = jnp.exp2(m_prev - m_new)
        pt = jnp.exp2(st - m_new)
        l_new = alpha * l_prev + jnp.sum(pt, axis=0, keepdims=True)
        vt = vt_ref[0, j, :, pl.ds(start, TK)]
        return m_new, l_new, alpha, _dot(vt, pt.astype(jnp.bfloat16))

    @pl.when(qi == 0)
    def _():
        for j in range(HB):
            park(j, scores(qt_ref, 0, j))

    def body(t, carry):
        out = []
        for j in range(HB):
            m_prev, l_prev = carry[j]
            m_new, l_new, alpha, pv = consume(t, j, s_ref[j], (m_prev, l_prev), cm_ref[j])
            park(j, scores(qt_ref, t + 1, j))
            acc_ref[j] = alpha * acc_ref[j] + pv
            out.append((m_new, l_new))
        return tuple(out)

    init = tuple((jnp.full((1, TQ), NEG_BIG, jnp.float32), jnp.zeros((1, TQ), jnp.float32))
                 for _ in range(HB))
    carry = lax.fori_loop(0, qi, body, init)

    kc = lax.broadcasted_iota(jnp.int32, (LB, LB), 0) // CHUNK
    qc = lax.broadcasted_iota(jnp.int32, (LB, LB), 1) // CHUNK
    diag_ok = kc <= qc
    start = pl.multiple_of(qi * TK, TK)
    for j in range(HB):
        m_prev, l_prev = carry[j]
        p_blocks, m_blocks, l_blocks = [], [], []
        for c in range(TQ // LB):
            lanes = slice(c * LB, (c + 1) * LB)
            sq = jnp.where(diag_ok, s_ref[j, c * LB:(c + 1) * LB, lanes], NEG_BIG)
            if c:
                st = jnp.concatenate([s_ref[j, 0:c * LB, lanes], sq], axis=0)
            else:
                st = sq
            m_new = jnp.maximum(m_prev[:, lanes], jnp.max(st, axis=0, keepdims=True))
            pt = jnp.exp2(st - m_new)
            m_blocks.append(m_new)
            l_blocks.append(jnp.sum(pt, axis=0, keepdims=True))
            pt = pt.astype(jnp.bfloat16)
            if (c + 1) * LB < TK:
                pt = jnp.concatenate(
                    [pt, jnp.zeros((TK - (c + 1) * LB, LB), jnp.bfloat16)], axis=0)
            p_blocks.append(pt)
        m_new = jnp.concatenate(m_blocks, axis=1)
        alpha = jnp.exp2(m_prev - m_new)
        l_new = alpha * l_prev + jnp.concatenate(l_blocks, axis=1)
        pv = _dot(vt_ref[0, j, :, pl.ds(start, TK)], jnp.concatenate(p_blocks, axis=1))
        park(j, scores(qtn_ref, 0, j))
        ot = (alpha * acc_ref[j] + pv) * (1.0 / l_new)
        o_ref[:, j * DV:(j + 1) * DV] = ot.T.astype(o_ref.dtype)


def _attn_call(qt, k, vt, batch, seq):
    assert TQ == TK
    nq = seq // TQ
    return pl.pallas_call(
        _attn_kernel,
        grid=(batch, HEADS // HB, nq),
        in_specs=[
            pl.BlockSpec((1, HB, DQK, TQ), lambda b, h, i: (b, h, 0, i)),
            pl.BlockSpec((1, HB, DQK, TQ), lambda b, h, i: (b, h, 0, jnp.minimum(i + 1, nq - 1))),
            pl.BlockSpec((1, HB, seq, DQK), lambda b, h, i: (b, h, 0, 0),
                         pipeline_mode=pl.Buffered(1)),
            pl.BlockSpec((1, HB, DV, seq), lambda b, h, i: (b, h, 0, 0),
                         pipeline_mode=pl.Buffered(1)),
        ],
        out_specs=pl.BlockSpec((TQ, HB * DV), lambda b, h, i: (b * nq + i, h)),
        out_shape=jax.ShapeDtypeStruct((batch * seq, MLA_W), jnp.bfloat16),
        scratch_shapes=[pltpu.VMEM((HB, TK, TQ), jnp.float32),
                        pltpu.VMEM((HB, 1, TQ), jnp.float32),
                        pltpu.VMEM((HB, DV, TQ), jnp.float32)],
        compiler_params=pltpu.CompilerParams(
            dimension_semantics=("parallel", "parallel", "arbitrary"),
            vmem_limit_bytes=VMEM_LIMIT),
        name="mla_attn",
    )(qt, qt, k, vt)


def _out_kernel(x_ref, o_ref, gate_ref, og_ref, omg_ref, wo_ref, fg_ref, out_ref):
    for r0 in range(0, TM_OUT, SUB_OUT):
        rows = slice(r0, r0 + SUB_OUT)
        h = x_ref[rows, :] + _dot(og_ref[rows, :], wo_ref[MLA_W:, :])
        om = _rms(o_ref[rows, :].astype(jnp.float32), omg_ref[...]) \
            * gate_ref[rows, :].astype(jnp.float32)
        h = h + _dot(om.astype(jnp.bfloat16), wo_ref[0:MLA_W, :])
        out_ref[rows, :] = _rms(h, fg_ref[...])


def _out_call(x2, o_mla, gate, og, omg, wo, fg):
    n = x2.shape[0]
    tm = TM_OUT
    row_spec = pl.BlockSpec((tm, D_MODEL), lambda i: (i, 0))
    return pl.pallas_call(
        _out_kernel,
        grid=(n // tm,),
        in_specs=[row_spec, row_spec, row_spec, row_spec,
                  _const_spec(omg.shape), _const_spec(wo.shape), _const_spec(fg.shape)],
        out_specs=row_spec,
        out_shape=jax.ShapeDtypeStruct((n, D_MODEL), jnp.float32),
        compiler_params=pltpu.CompilerParams(
            dimension_semantics=("parallel",), vmem_limit_bytes=VMEM_LIMIT),
        name="out_proj",
    )(x2, o_mla, gate, og, omg, wo, fg)


def _rope_tables(seq):
    pos = jnp.arange(seq, dtype=jnp.float32)
    inv_freq = ROPE_THETA ** (-jnp.arange(0, ROPE, 2, dtype=jnp.float32) / ROPE)
    ang = pos[:, None] * inv_freq[None, :]
    cos, sin = jnp.cos(ang), jnp.sin(ang)
    ones = jnp.ones((seq, NOPE), jnp.float32)
    zeros_n = jnp.zeros((seq, NOPE), jnp.float32)
    pad = jnp.zeros((seq, DQK - NOPE - ROPE), jnp.float32)
    cos_t = jnp.concatenate([ones, cos, cos, pad], axis=1)
    sin_t = jnp.concatenate([zeros_n, sin, sin, pad], axis=1)
    scale = math.log2(math.e) / math.sqrt(NOPE + ROPE)
    return cos_t * scale, sin_t * scale, cos_t, sin_t


def _rot_half_cols(w):
    half = ROPE // 2
    return jnp.concatenate([-w[..., half:], w[..., :half]], axis=-1)


def _pack_weights(w_in, w_uq, w_ukv):
    bf = jnp.bfloat16
    d = w_in.shape[0]
    kr_w = w_in[:, L_KR:L_KR + ROPE]
    wlat = jnp.concatenate([w_in[:, :L_KR], jnp.zeros((d, NOPE), w_in.dtype), kr_w,
                            _rot_half_cols(kr_w)], axis=1).astype(bf)
    wbig = w_in[:, L_KR + ROPE:].astype(bf)
    wq = w_uq.reshape(Q_RANK, HEADS, NOPE + ROPE)
    wuq = jnp.concatenate([wq, _rot_half_cols(wq[..., NOPE:])],
                          axis=-1).reshape(Q_RANK, HEADS * DQK).astype(bf)
    wkv = w_ukv.reshape(KV_RANK, HEADS, NOPE + DV)
    wkn = jnp.concatenate([wkv[..., :NOPE], jnp.zeros((KV_RANK, HEADS, DQK - NOPE), w_ukv.dtype)],
                          axis=-1).reshape(KV_RANK, HEADS * DQK).astype(bf)
    wv = wkv[..., NOPE:].reshape(KV_RANK, HEADS * DV).astype(bf)
    return wlat, wbig, wuq, wkn, wv


def kernel(x, norm_in_g, w_in, q_norm_g, w_uq, kv_norm_g, w_ukv, gmlp_ln_g, gmlp_ln_b,
           w_spatial, b_spatial, out_norm_mla_g, out_norm_gmlp_g, w_out, final_norm_g):
    batch, seq, d = x.shape
    n = batch * seq
    x2 = x.reshape(n, d)
    row = lambda a: a.reshape(1, -1).astype(jnp.float32)

    wlat, wbig, wuq, wkn, wv = _pack_weights(w_in, w_uq, w_ukv)
    ws = w_spatial.astype(jnp.float32)
    bs = jnp.broadcast_to(b_spatial[:, :, None], (GROUPS, SBLOCK, GDIM)).astype(jnp.float32)
    consts = (row(norm_in_g), wlat, wbig, row(q_norm_g), wuq, row(kv_norm_g), wkn, wv,
              row(gmlp_ln_g), row(gmlp_ln_b), ws, bs, row(out_norm_gmlp_g))

    qt, k, vt, gate, og = _proj_call(x2, _rope_tables(seq), consts, batch, seq)
    o_mla = _attn_call(qt, k, vt, batch, seq)
    out = _out_call(x2, o_mla, gate, og, row(out_norm_mla_g), w_out.astype(jnp.bfloat16),
                    row(final_norm_g))
    return out.reshape(batch, seq, d)
```

```python
import math

import jax
import jax.numpy as jnp
from jax import lax
from jax.experimental import pallas as pl
from jax.experimental.pallas import tpu as pltpu

D_MODEL = 1024
CHUNK = 64
EPS = 1e-6

HEADS = 8
NOPE = 64
ROPE = 32
DV = 128
DQK = 128
Q_RANK = 384
KV_RANK = 256
MLA_W = HEADS * DV
ROPE_THETA = 10000.0

GROUPS = 8
GDIM = 128
GMLP_W = GROUPS * GDIM
SBLOCK = 128

L_QLAT = 0
L_KVLAT = L_QLAT + Q_RANK
L_KR = L_KVLAT + KV_RANK
D_LAT = L_KR + DQK
B_GMLA = 0
B_U = B_GMLA + MLA_W
B_V = B_U + GMLP_W
B_GG = B_V + GMLP_W
D_BIG = B_GG + GMLP_W

VMEM_LIMIT = 56 * 1024 * 1024

TM_PROJ = 512
SUB_PROJ = 256
TQ = 512
TK = 512
HB = 8
LB = 128
TM_OUT = 512
PACK_ROWS = 128
POS_SPLIT = 128
SUB_OUT = 256

NEG_BIG = -1e30


def _rms(x, g):
    return x * lax.rsqrt(jnp.mean(x * x, axis=-1, keepdims=True) + EPS) * g


def _dot(a, b):
    return jnp.dot(a, b, preferred_element_type=jnp.float32)


def _silu(x):
    return x * (1.0 / (1.0 + jnp.exp(-x)))


def _gelu(x):
    return 0.5 * x * (1.0 + lax.erf(x * (1.0 / math.sqrt(2.0))))


def _rope(x, cos, sin):
    return x * cos + pltpu.roll(x, DQK - ROPE, 1) * sin


def _proj_kernel(x_ref, cosq_ref, sinq_ref, cosk_ref, sink_ref, ng_ref, wlat_ref, wbig_ref,
                 qg_ref, wuq_ref, kvg_ref, wkn_ref, wv_ref, lng_ref, lnb_ref, ws_ref, bs_ref,
                 ogg_ref, qt_ref, k_ref, vt_ref, gate_ref, og_ref):
    tc = lax.broadcasted_iota(jnp.int32, (SBLOCK, SBLOCK), 0) // CHUNK
    sc = lax.broadcasted_iota(jnp.int32, (SBLOCK, SBLOCK), 1) // CHUNK
    ws = [jnp.where(sc <= tc, ws_ref[g], 0.0).astype(jnp.bfloat16) for g in range(GROUPS)]
    nblk = SUB_PROJ // SBLOCK

    for r0 in range(0, TM_PROJ, SUB_PROJ):
        rows = slice(r0, r0 + SUB_PROJ)
        y = _rms(x_ref[rows, :], ng_ref[...]).astype(jnp.bfloat16)

        v = _gelu(_dot(y, wbig_ref[:, B_V:B_GG]))
        u = _gelu(_dot(y, wbig_ref[:, B_U:B_V]))
        mu = jnp.mean(v, axis=-1, keepdims=True)
        vc = v - mu
        var = jnp.mean(vc * vc, axis=-1, keepdims=True)
        vln = (vc * lax.rsqrt(var + EPS) * lng_ref[...] + lnb_ref[...]).astype(jnp.bfloat16)

        lat = _dot(y, wlat_ref[...])
        gate_ref[rows, :] = _silu(_dot(y, wbig_ref[:, B_GMLA:B_U])).astype(jnp.bfloat16)

        mixed_g = []
        for g in range(GROUPS):
            cols = slice(g * GDIM, (g + 1) * GDIM)
            vb = jnp.concatenate([vln[c * SBLOCK:(c + 1) * SBLOCK, cols] for c in range(nblk)],
                                 axis=1)
            mixed_g.append(_dot(ws[g], vb))
        mixed = jnp.concatenate(
            [jnp.concatenate([mixed_g[g][:, c * GDIM:(c + 1) * GDIM] + bs_ref[g]
                              for g in range(GROUPS)], axis=1) for c in range(nblk)], axis=0)
        ggs = _silu(_dot(y, wbig_ref[:, B_GG:D_BIG]))
        og_ref[rows, :] = (_rms(u * mixed, ogg_ref[...]) * ggs).astype(jnp.bfloat16)

        qn = _rms(lat[:, L_QLAT:L_KVLAT], qg_ref[...]).astype(jnp.bfloat16)
        kvn = _rms(lat[:, L_KVLAT:L_KR], kvg_ref[...]).astype(jnp.bfloat16)
        kr = _rope(lat[:, L_KR:D_LAT], cosk_ref[rows, :], sink_ref[rows, :])
        q_all = _dot(qn, wuq_ref[...])
        k_all = _dot(kvn, wkn_ref[...])
        v_all = _dot(kvn, wv_ref[...])
        cosq = cosq_ref[rows, :]
        sinq = sinq_ref[rows, :]
        for h in range(HEADS):
            sl = slice(h * DQK, (h + 1) * DQK)
            qt_ref[0, h, :, rows] = _rope(q_all[:, sl], cosq, sinq).T.astype(jnp.bfloat16)
            k_ref[0, h, rows, :] = (k_all[:, sl] + kr).astype(jnp.bfloat16)
            vt_ref[0, h, :, rows] = v_all[:, sl].T.astype(jnp.bfloat16)


def _const_spec(shape):
    nd = len(shape)
    return pl.BlockSpec(shape, lambda i, _nd=nd: (0,) * _nd, pipeline_mode=pl.Buffered(1))


def _proj_call(x2, tables, consts, batch, seq):
    n = x2.shape[0]
    tm = TM_PROJ
    tiles_per_seq = seq // tm
    row_spec = pl.BlockSpec((tm, D_MODEL), lambda i: (i, 0))
    tab_spec = pl.BlockSpec((tm, DQK), lambda i: (i % tiles_per_seq, 0))
    head_spec = pl.BlockSpec((1, HEADS, tm, DQK),
                             lambda i: (i // tiles_per_seq, 0, i % tiles_per_seq, 0))
    head_shape = jax.ShapeDtypeStruct((batch, HEADS, seq, DQK), jnp.bfloat16)
    headt_spec = pl.BlockSpec((1, HEADS, DQK, tm),
                              lambda i: (i // tiles_per_seq, 0, 0, i % tiles_per_seq))
    headt_shape = jax.ShapeDtypeStruct((batch, HEADS, DQK, seq), jnp.bfloat16)
    row_shape = jax.ShapeDtypeStruct((n, D_MODEL), jnp.bfloat16)
    return pl.pallas_call(
        _proj_kernel,
        grid=(n // tm,),
        in_specs=[row_spec] + [tab_spec] * len(tables) + [_const_spec(c.shape) for c in consts],
        out_specs=[headt_spec, head_spec, headt_spec, row_spec, row_spec],
        out_shape=[headt_shape, head_shape, headt_shape, row_shape, row_shape],
        compiler_params=pltpu.CompilerParams(
            dimension_semantics=("parallel",), vmem_limit_bytes=VMEM_LIMIT),
        name="proj_gmlp",
    )(x2, *tables, *consts)


def _attn_kernel(qt_ref, qtn_ref, k_ref, vt_ref, o_ref, s_ref, cm_ref, acc_ref):
    qi = pl.program_id(2)
    acc_ref[...] = jnp.zeros_like(acc_ref)

    def scores(q_ref, t, j):
        start = pl.multiple_of(t * TK, TK)
        return _dot(k_ref[0, j, pl.ds(start, TK), :], q_ref[0, j])

    def park(j, st):
        s_ref[j] = st
        cm_ref[j] = jnp.max(st, axis=0, keepdims=True)

    def consume(t, j, st, state, cmax):
        m_prev, l_prev = state
        start = pl.multiple_of(t * TK, TK)
        m_new = jnp.maximum(m_prev, cmax)
        alpha = jnp.exp2(m_prev - m_new)
        pt = jnp.exp2(st - m_new)
        l_new = alpha * l_prev + jnp.sum(pt, axis=0, keepdims=True)
        vt = vt_ref[0, j, :, pl.ds(start, TK)]
        return m_new, l_new, alpha, _dot(vt, pt.astype(jnp.bfloat16))

    @pl.when(qi == 0)
    def _():
        for j in range(HB):
            park(j, scores(qt_ref, 0, j))

    def body(t, carry):
        out = []
        for j in range(HB):
            m_prev, l_prev = carry[j]
            m_new, l_new, alpha, pv = consume(t, j, s_ref[j], (m_prev, l_prev), cm_ref[j])
            park(j, scores(qt_ref, t + 1, j))
            acc_ref[j] = alpha * acc_ref[j] + pv
            out.append((m_new, l_new))
        return tuple(out)

    init = tuple((jnp.full((1, TQ), NEG_BIG, jnp.float32), jnp.zeros((1, TQ), jnp.float32))
                 for _ in range(HB))
    carry = lax.fori_loop(0, qi // 2, lambda u, c: body(2 * u + 1, body(2 * u, c)), init)
    carry = lax.cond(qi % 2 == 1, lambda c: body(qi - 1, c), lambda c: c, carry)

    kc = lax.broadcasted_iota(jnp.int32, (LB, LB), 0) // CHUNK
    qc = lax.broadcasted_iota(jnp.int32, (LB, LB), 1) // CHUNK
    diag_ok = kc <= qc
    start = pl.multiple_of(qi * TK, TK)
    for j in range(HB):
        m_prev, l_prev = carry[j]
        p_blocks, m_blocks, l_blocks = [], [], []
        for c in range(TQ // LB):
            lanes = slice(c * LB, (c + 1) * LB)
            sq = jnp.where(diag_ok, s_ref[j, c * LB:(c + 1) * LB, lanes], NEG_BIG)
            if c:
                st = jnp.concatenate([s_ref[j, 0:c * LB, lanes], sq], axis=0)
            else:
                st = sq
            m_new = jnp.maximum(m_prev[:, lanes], jnp.max(st, axis=0, keepdims=True))
            pt = jnp.exp2(st - m_new)
            m_blocks.append(m_new)
            l_blocks.append(jnp.sum(pt, axis=0, keepdims=True))
            pt = pt.astype(jnp.bfloat16)
            if (c + 1) * LB < TK:
                pt = jnp.concatenate(
                    [pt, jnp.zeros((TK - (c + 1) * LB, LB), jnp.bfloat16)], axis=0)
            p_blocks.append(pt)
        m_new = jnp.concatenate(m_blocks, axis=1)
        alpha = jnp.exp2(m_prev - m_new)
        l_new = alpha * l_prev + jnp.concatenate(l_blocks, axis=1)
        pv = _dot(vt_ref[0, j, :, pl.ds(start, TK)], jnp.concatenate(p_blocks, axis=1))
        park(j, scores(qtn_ref, 0, j))
        ot = (alpha * acc_ref[j] + pv) * (1.0 / l_new)
        o_ref[:, j * DV:(j + 1) * DV] = ot.T.astype(o_ref.dtype)


def _attn_call(qt, k, vt, batch, seq):
    assert TQ == TK
    nq = seq // TQ
    return pl.pallas_call(
        _attn_kernel,
        grid=(batch, HEADS // HB, nq),
        in_specs=[
            pl.BlockSpec((1, HB, DQK, TQ), lambda b, h, i: (b, h, 0, i)),
            pl.BlockSpec((1, HB, DQK, TQ), lambda b, h, i: (b, h, 0, jnp.minimum(i + 1, nq - 1))),
            pl.BlockSpec((1, HB, seq, DQK), lambda b, h, i: (b, h, 0, 0),
                         pipeline_mode=pl.Buffered(1)),
            pl.BlockSpec((1, HB, DV, seq), lambda b, h, i: (b, h, 0, 0),
                         pipeline_mode=pl.Buffered(1)),
        ],
        out_specs=pl.BlockSpec((TQ, HB * DV), lambda b, h, i: (b * nq + i, h)),
        out_shape=jax.ShapeDtypeStruct((batch * seq, MLA_W), jnp.bfloat16),
        scratch_shapes=[pltpu.VMEM((HB, TK, TQ), jnp.float32),
                        pltpu.VMEM((HB, 1, TQ), jnp.float32),
                        pltpu.VMEM((HB, DV, TQ), jnp.float32)],
        compiler_params=pltpu.CompilerParams(
            dimension_semantics=("parallel", "parallel", "arbitrary"),
            vmem_limit_bytes=VMEM_LIMIT),
        name="mla_attn",
    )(qt, qt, k, vt)


def _out_kernel(x_ref, o_ref, gate_ref, og_ref, omg_ref, wo_ref, fg_ref, out_ref):
    for r0 in range(0, TM_OUT, SUB_OUT):
        rows = slice(r0, r0 + SUB_OUT)
        h = x_ref[rows, :] + _dot(og_ref[rows, :], wo_ref[MLA_W:, :])
        om = _rms(o_ref[rows, :].astype(jnp.float32), omg_ref[...]) \
            * gate_ref[rows, :].astype(jnp.float32)
        h = h + _dot(om.astype(jnp.bfloat16), wo_ref[0:MLA_W, :])
        out_ref[rows, :] = _rms(h, fg_ref[...])


def _out_call(x2, o_mla, gate, og, omg, wo, fg):
    n = x2.shape[0]
    tm = TM_OUT
    row_spec = pl.BlockSpec((tm, D_MODEL), lambda i: (i, 0))
    return pl.pallas_call(
        _out_kernel,
        grid=(n // tm,),
        in_specs=[row_spec, row_spec, row_spec, row_spec,
                  _const_spec(omg.shape), _const_spec(wo.shape), _const_spec(fg.shape)],
        out_specs=row_spec,
        out_shape=jax.ShapeDtypeStruct((n, D_MODEL), jnp.float32),
        compiler_params=pltpu.CompilerParams(
            dimension_semantics=("parallel",), vmem_limit_bytes=VMEM_LIMIT),
        name="out_proj",
    )(x2, o_mla, gate, og, omg, wo, fg)


def _rope_tables(seq):
    inv_freq = ROPE_THETA ** (-jnp.arange(0, ROPE, 2, dtype=jnp.float32) / ROPE)
    hi = jnp.arange(0, seq, POS_SPLIT, dtype=jnp.float32)[:, None] * inv_freq[None, :]
    lo = jnp.arange(POS_SPLIT, dtype=jnp.float32)[:, None] * inv_freq[None, :]
    ch, sh = jnp.cos(hi)[:, None, :], jnp.sin(hi)[:, None, :]
    cl, sl = jnp.cos(lo)[None, :, :], jnp.sin(lo)[None, :, :]
    cos = (ch * cl - sh * sl).reshape(seq, ROPE // 2)
    sin = (sh * cl + ch * sl).reshape(seq, ROPE // 2)
    ones = jnp.ones((seq, NOPE), jnp.float32)
    zeros_n = jnp.zeros((seq, NOPE), jnp.float32)
    pad = jnp.zeros((seq, DQK - NOPE - ROPE), jnp.float32)
    cos_t = jnp.concatenate([ones, cos, cos, pad], axis=1)
    sin_t = jnp.concatenate([zeros_n, sin, sin, pad], axis=1)
    scale = math.log2(math.e) / math.sqrt(NOPE + ROPE)
    return cos_t * scale, sin_t * scale, cos_t, sin_t


def _rot_half_cols(w):
    half = ROPE // 2
    return jnp.concatenate([-w[..., half:], w[..., :half]], axis=-1)


def _realign_kernel(w_ref, wbig_ref):
    wbig_ref[...] = w_ref[:, L_KR + ROPE:].astype(jnp.bfloat16)


def _realign_call(w_in):
    d, d_in = w_in.shape
    rb = PACK_ROWS
    return pl.pallas_call(
        _realign_kernel,
        grid=(d // rb,),
        in_specs=[pl.BlockSpec((rb, d_in), lambda i: (i, 0))],
        out_specs=pl.BlockSpec((rb, D_BIG), lambda i: (i, 0)),
        out_shape=jax.ShapeDtypeStruct((d, D_BIG), jnp.bfloat16),
        compiler_params=pltpu.CompilerParams(dimension_semantics=("parallel",)),
        name="realign_w_in",
    )(w_in)


def _pack_weights(w_in, w_uq, w_ukv):
    bf = jnp.bfloat16
    d = w_in.shape[0]
    kr_w = w_in[:, L_KR:L_KR + ROPE]
    wlat = jnp.concatenate([w_in[:, :L_KR], jnp.zeros((d, NOPE), w_in.dtype), kr_w,
                            _rot_half_cols(kr_w)], axis=1).astype(bf)
    wbig = _realign_call(w_in)
    wq = w_uq.reshape(Q_RANK, HEADS, NOPE + ROPE)
    wuq = jnp.concatenate([wq, _rot_half_cols(wq[..., NOPE:])],
                          axis=-1).reshape(Q_RANK, HEADS * DQK).astype(bf)
    wkv = w_ukv.reshape(KV_RANK, HEADS, NOPE + DV)
    wkn = jnp.concatenate([wkv[..., :NOPE], jnp.zeros((KV_RANK, HEADS, DQK - NOPE), w_ukv.dtype)],
                          axis=-1).reshape(KV_RANK, HEADS * DQK).astype(bf)
    wv = wkv[..., NOPE:].reshape(KV_RANK, HEADS * DV).astype(bf)
    return wlat, wbig, wuq, wkn, wv


def kernel(x, norm_in_g, w_in, q_norm_g, w_uq, kv_norm_g, w_ukv, gmlp_ln_g, gmlp_ln_b,
           w_spatial, b_spatial, out_norm_mla_g, out_norm_gmlp_g, w_out, final_norm_g):
    batch, seq, d = x.shape
    n = batch * seq
    x2 = x.reshape(n, d)
    row = lambda a: a.reshape(1, -1).astype(jnp.float32)

    wlat, wbig, wuq, wkn, wv = _pack_weights(w_in, w_uq, w_ukv)
    ws = w_spatial.astype(jnp.float32)
    bs = jnp.broadcast_to(b_spatial[:, :, None], (GROUPS, SBLOCK, GDIM)).astype(jnp.float32)
    consts = (row(norm_in_g), wlat, wbig, row(q_norm_g), wuq, row(kv_norm_g), wkn, wv,
              row(gmlp_ln_g), row(gmlp_ln_b), ws, bs, row(out_norm_gmlp_g))

    qt, k, vt, gate, og = _proj_call(x2, _rope_tables(seq), consts, batch, seq)
    o_mla = _attn_call(qt, k, vt, batch, seq)
    out = _out_call(x2, o_mla, gate, og, row(out_norm_mla_g), w_out.astype(jnp.bfloat16),
                    row(final_norm_g))
    return out.reshape(batch, seq, d)
```

```python
import math

import jax
import jax.numpy as jnp
from jax import lax
from jax.experimental import pallas as pl
from jax.experimental.pallas import tpu as pltpu

D_MODEL = 1024
CHUNK = 64
EPS = 1e-6

HEADS = 8
NOPE = 64
ROPE = 32
DV = 128
DQK = 128
Q_RANK = 384
KV_RANK = 256
MLA_W = HEADS * DV
ROPE_THETA = 10000.0

GROUPS = 8
GDIM = 128
GMLP_W = GROUPS * GDIM
SBLOCK = 128

L_QLAT = 0
L_KVLAT = L_QLAT + Q_RANK
L_KR = L_KVLAT + KV_RANK
D_LAT = L_KR + DQK
B_GMLA = 0
B_U = B_GMLA + MLA_W
B_V = B_U + GMLP_W
B_GG = B_V + GMLP_W
D_BIG = B_GG + GMLP_W

VMEM_LIMIT = 56 * 1024 * 1024

TM_PROJ = 512
SUB_PROJ = 256
TQ = 512
TK = 512
HB = 8
LB = 128
TM_OUT = 512
PACK_COLS = 512
POS_SPLIT = 128
SUB_OUT = 256

NEG_BIG = -1e30


def _rms(x, g):
    return x * lax.rsqrt(jnp.mean(x * x, axis=-1, keepdims=True) + EPS) * g


def _dot(a, b):
    return jnp.dot(a, b, preferred_element_type=jnp.float32)


def _silu(x):
    return x * (1.0 / (1.0 + jnp.exp(-x)))


def _gelu(x):
    return 0.5 * x * (1.0 + lax.erf(x * (1.0 / math.sqrt(2.0))))


def _rope(x, cos, sin):
    return x * cos + pltpu.roll(x, DQK - ROPE, 1) * sin


def _proj_kernel(x_ref, cosq_ref, sinq_ref, cosk_ref, sink_ref, ng_ref, wlat_ref, wbig_ref,
                 qg_ref, wuq_ref, kvg_ref, wkn_ref, wv_ref, lng_ref, lnb_ref, ws_ref, bs_ref,
                 ogg_ref, qt_ref, k_ref, vt_ref, gate_ref, og_ref):
    tc = lax.broadcasted_iota(jnp.int32, (SBLOCK, SBLOCK), 0) // CHUNK
    sc = lax.broadcasted_iota(jnp.int32, (SBLOCK, SBLOCK), 1) // CHUNK
    ws = [jnp.where(sc <= tc, ws_ref[g], 0.0).astype(jnp.bfloat16) for g in range(GROUPS)]
    nblk = SUB_PROJ // SBLOCK

    for r0 in range(0, TM_PROJ, SUB_PROJ):
        rows = slice(r0, r0 + SUB_PROJ)
        y = _rms(x_ref[rows, :], ng_ref[...]).astype(jnp.bfloat16)

        v = _gelu(_dot(y, wbig_ref[:, B_V:B_GG]))
        u = _gelu(_dot(y, wbig_ref[:, B_U:B_V]))
        mu = jnp.mean(v, axis=-1, keepdims=True)
        vc = v - mu
        var = jnp.mean(vc * vc, axis=-1, keepdims=True)
        vln = (vc * lax.rsqrt(var + EPS) * lng_ref[...] + lnb_ref[...]).astype(jnp.bfloat16)

        lat = _dot(y, wlat_ref[...])
        gate_ref[rows, :] = _silu(_dot(y, wbig_ref[:, B_GMLA:B_U])).astype(jnp.bfloat16)
        ggs = _silu(_dot(y, wbig_ref[:, B_GG:D_BIG]))

        mixed_g = []
        for g in range(GROUPS):
            cols = slice(g * GDIM, (g + 1) * GDIM)
            vb = jnp.concatenate([vln[c * SBLOCK:(c + 1) * SBLOCK, cols] for c in range(nblk)],
                                 axis=1)
            mixed_g.append(_dot(ws[g], vb))
        mixed = jnp.concatenate(
            [jnp.concatenate([mixed_g[g][:, c * GDIM:(c + 1) * GDIM] + bs_ref[g]
                              for g in range(GROUPS)], axis=1) for c in range(nblk)], axis=0)
        og_ref[rows, :] = (_rms(u * mixed, ogg_ref[...]) * ggs).astype(jnp.bfloat16)

        qn = _rms(lat[:, L_QLAT:L_KVLAT], qg_ref[...]).astype(jnp.bfloat16)
        kvn = _rms(lat[:, L_KVLAT:L_KR], kvg_ref[...]).astype(jnp.bfloat16)
        kr = _rope(lat[:, L_KR:D_LAT], cosk_ref[rows, :], sink_ref[rows, :])
        q_all = _dot(qn, wuq_ref[...])
        k_all = _dot(kvn, wkn_ref[...])
        v_all = _dot(kvn, wv_ref[...])
        cosq = cosq_ref[rows, :]
        sinq = sinq_ref[rows, :]
        for h in range(HEADS):
            sl = slice(h * DQK, (h + 1) * DQK)
            qt_ref[0, h, :, rows] = _rope(q_all[:, sl], cosq, sinq).T.astype(jnp.bfloat16)
            k_ref[0, h, rows, :] = (k_all[:, sl] + kr).astype(jnp.bfloat16)
            vt_ref[0, h, :, rows] = v_all[:, sl].T.astype(jnp.bfloat16)


def _const_spec(shape):
    nd = len(shape)
    return pl.BlockSpec(shape, lambda i, _nd=nd: (0,) * _nd, pipeline_mode=pl.Buffered(1))


def _proj_call(x2, tables, consts, batch, seq):
    n = x2.shape[0]
    tm = TM_PROJ
    tiles_per_seq = seq // tm
    row_spec = pl.BlockSpec((tm, D_MODEL), lambda i: (i, 0))
    tab_spec = pl.BlockSpec((tm, DQK), lambda i: (i % tiles_per_seq, 0))
    head_spec = pl.BlockSpec((1, HEADS, tm, DQK),
                             lambda i: (i // tiles_per_seq, 0, i % tiles_per_seq, 0))
    head_shape = jax.ShapeDtypeStruct((batch, HEADS, seq, DQK), jnp.bfloat16)
    headt_spec = pl.BlockSpec((1, HEADS, DQK, tm),
                              lambda i: (i // tiles_per_seq, 0, 0, i % tiles_per_seq))
    headt_shape = jax.ShapeDtypeStruct((batch, HEADS, DQK, seq), jnp.bfloat16)
    row_shape = jax.ShapeDtypeStruct((n, D_MODEL), jnp.bfloat16)
    return pl.pallas_call(
        _proj_kernel,
        grid=(n // tm,),
        in_specs=[row_spec] + [tab_spec] * len(tables) + [_const_spec(c.shape) for c in consts],
        out_specs=[headt_spec, head_spec, headt_spec, row_spec, row_spec],
        out_shape=[headt_shape, head_shape, headt_shape, row_shape, row_shape],
        compiler_params=pltpu.CompilerParams(
            dimension_semantics=("parallel",), vmem_limit_bytes=VMEM_LIMIT),
        name="proj_gmlp",
    )(x2, *tables, *consts)


def _attn_kernel(qt_ref, qtn_ref, k_ref, vt_ref, o_ref, s_ref, cm_ref, acc_ref):
    qi = pl.program_id(2)
    acc_ref[...] = jnp.zeros_like(acc_ref)

    def scores(q_ref, t, j):
        start = pl.multiple_of(t * TK, TK)
        return _dot(k_ref[0, j, pl.ds(start, TK), :], q_ref[0, j])

    def park(j, st):
        s_ref[j] = st
        cm_ref[j] = jnp.max(st, axis=0, keepdims=True)

    def consume(t, j, st, state, cmax):
        m_prev, l_prev = state
        start = pl.multiple_of(t * TK, TK)
        m_new = jnp.maximum(m_prev, cmax)
        alpha = jnp.exp2(m_prev - m_new)
        pt = jnp.exp2(st - m_new)
        l_new = alpha * l_prev + jnp.sum(pt, axis=0, keepdims=True)
        vt = vt_ref[0, j, :, pl.ds(start, TK)]
        return m_new, l_new, alpha, _dot(vt, pt.astype(jnp.bfloat16))

    @pl.when(qi == 0)
    def _():
        for j in range(HB):
            park(j, scores(qt_ref, 0, j))

    def body(t, carry):
        out = []
        for j in range(HB):
            m_prev, l_prev = carry[j]
            m_new, l_new, alpha, pv = consume(t, j, s_ref[j], (m_prev, l_prev), cm_ref[j])
            park(j, scores(qt_ref, t + 1, j))
            acc_ref[j] = alpha * acc_ref[j] + pv
            out.append((m_new, l_new))
        return tuple(out)

    init = tuple((jnp.full((1, TQ), NEG_BIG, jnp.float32), jnp.zeros((1, TQ), jnp.float32))
                 for _ in range(HB))
    carry = lax.fori_loop(0, qi // 2, lambda u, c: body(2 * u + 1, body(2 * u, c)), init)
    carry = lax.cond(qi % 2 == 1, lambda c: body(qi - 1, c), lambda c: c, carry)

    kc = lax.broadcasted_iota(jnp.int32, (LB, LB), 0) // CHUNK
    qc = lax.broadcasted_iota(jnp.int32, (LB, LB), 1) // CHUNK
    diag_ok = kc <= qc
    start = pl.multiple_of(qi * TK, TK)
    for j in range(HB):
        m_prev, l_prev = carry[j]
        p_blocks, m_blocks, l_blocks = [], [], []
        for c in range(TQ // LB):
            lanes = slice(c * LB, (c + 1) * LB)
            sq = jnp.where(diag_ok, s_ref[j, c * LB:(c + 1) * LB, lanes], NEG_BIG)
            if c:
                st = jnp.concatenate([s_ref[j, 0:c * LB, lanes], sq], axis=0)
            else:
                st = sq
            m_new = jnp.maximum(m_prev[:, lanes], jnp.max(st, axis=0, keepdims=True))
            pt = jnp.exp2(st - m_new)
            m_blocks.append(m_new)
            l_blocks.append(jnp.sum(pt, axis=0, keepdims=True))
            pt = pt.astype(jnp.bfloat16)
            if (c + 1) * LB < TK:
                pt = jnp.concatenate(
                    [pt, jnp.zeros((TK - (c + 1) * LB, LB), jnp.bfloat16)], axis=0)
            p_blocks.append(pt)
        m_new = jnp.concatenate(m_blocks, axis=1)
        alpha = jnp.exp2(m_prev - m_new)
        l_new = alpha * l_prev + jnp.concatenate(l_blocks, axis=1)
        pv = _dot(vt_ref[0, j, :, pl.ds(start, TK)], jnp.concatenate(p_blocks, axis=1))
        park(j, scores(qtn_ref, 0, j))
        ot = (alpha * acc_ref[j] + pv) * (1.0 / l_new)
        o_ref[:, j * DV:(j + 1) * DV] = ot.T.astype(o_ref.dtype)


def _attn_call(qt, k, vt, batch, seq):
    assert TQ == TK
    nq = seq // TQ
    return pl.pallas_call(
        _attn_kernel,
        grid=(batch, HEADS // HB, nq),
        in_specs=[
            pl.BlockSpec((1, HB, DQK, TQ), lambda b, h, i: (b, h, 0, i)),
            pl.BlockSpec((1, HB, DQK, TQ), lambda b, h, i: (b, h, 0, jnp.minimum(i + 1, nq - 1))),
            pl.BlockSpec((1, HB, seq, DQK), lambda b, h, i: (b, h, 0, 0),
                         pipeline_mode=pl.Buffered(1)),
            pl.BlockSpec((1, HB, DV, seq), lambda b, h, i: (b, h, 0, 0),
                         pipeline_mode=pl.Buffered(1)),
        ],
        out_specs=pl.BlockSpec((TQ, HB * DV), lambda b, h, i: (b * nq + i, h)),
        out_shape=jax.ShapeDtypeStruct((batch * seq, MLA_W), jnp.bfloat16),
        scratch_shapes=[pltpu.VMEM((HB, TK, TQ), jnp.float32),
                        pltpu.VMEM((HB, 1, TQ), jnp.float32),
                        pltpu.VMEM((HB, DV, TQ), jnp.float32)],
        compiler_params=pltpu.CompilerParams(
            dimension_semantics=("parallel", "parallel", "arbitrary"),
            vmem_limit_bytes=VMEM_LIMIT),
        name="mla_attn",
    )(qt, qt, k, vt)


def _out_kernel(x_ref, o_ref, gate_ref, og_ref, omg_ref, wo_ref, fg_ref, out_ref):
    for r0 in range(0, TM_OUT, SUB_OUT):
        rows = slice(r0, r0 + SUB_OUT)
        h = x_ref[rows, :] + _dot(og_ref[rows, :], wo_ref[MLA_W:, :])
        om = _rms(o_ref[rows, :].astype(jnp.float32), omg_ref[...]) \
            * gate_ref[rows, :].astype(jnp.float32)
        h = h + _dot(om.astype(jnp.bfloat16), wo_ref[0:MLA_W, :])
        out_ref[rows, :] = _rms(h, fg_ref[...])


def _out_call(x2, o_mla, gate, og, omg, wo, fg):
    n = x2.shape[0]
    tm = TM_OUT
    row_spec = pl.BlockSpec((tm, D_MODEL), lambda i: (i, 0))
    return pl.pallas_call(
        _out_kernel,
        grid=(n // tm,),
        in_specs=[row_spec, row_spec, row_spec, row_spec,
                  _const_spec(omg.shape), _const_spec(wo.shape), _const_spec(fg.shape)],
        out_specs=row_spec,
        out_shape=jax.ShapeDtypeStruct((n, D_MODEL), jnp.float32),
        compiler_params=pltpu.CompilerParams(
            dimension_semantics=("parallel",), vmem_limit_bytes=VMEM_LIMIT),
        name="out_proj",
    )(x2, o_mla, gate, og, omg, wo, fg)


def _rope_tables(seq):
    inv_freq = ROPE_THETA ** (-jnp.arange(0, ROPE, 2, dtype=jnp.float32) / ROPE)
    hi = jnp.arange(0, seq, POS_SPLIT, dtype=jnp.float32)[:, None] * inv_freq[None, :]
    lo = jnp.arange(POS_SPLIT, dtype=jnp.float32)[:, None] * inv_freq[None, :]
    ch, sh = jnp.cos(hi)[:, None, :], jnp.sin(hi)[:, None, :]
    cl, sl = jnp.cos(lo)[None, :, :], jnp.sin(lo)[None, :, :]
    cos = (ch * cl - sh * sl).reshape(seq, ROPE // 2)
    sin = (sh * cl + ch * sl).reshape(seq, ROPE // 2)
    ones = jnp.ones((seq, NOPE), jnp.float32)
    zeros_n = jnp.zeros((seq, NOPE), jnp.float32)
    pad = jnp.zeros((seq, DQK - NOPE - ROPE), jnp.float32)
    cos_t = jnp.concatenate([ones, cos, cos, pad], axis=1)
    sin_t = jnp.concatenate([zeros_n, sin, sin, pad], axis=1)
    scale = math.log2(math.e) / math.sqrt(NOPE + ROPE)
    return cos_t * scale, sin_t * scale, cos_t, sin_t


def _rot_half_cols(w):
    half = ROPE // 2
    return jnp.concatenate([-w[..., half:], w[..., :half]], axis=-1)


def _pack_big_kernel(wt_ref, wbig_ref):
    wbig_ref[...] = wt_ref[...].T.astype(jnp.bfloat16)


def _pack_lat_kernel(wt_ref, wlat_ref):
    w = wt_ref[...]
    kr = w[L_KR:L_KR + ROPE]
    half = ROPE // 2
    blk = jnp.concatenate([w[:L_KR], jnp.zeros((NOPE, w.shape[1]), w.dtype), kr,
                           -kr[half:], kr[:half]], axis=0)
    wlat_ref[...] = blk.T.astype(jnp.bfloat16)


def _pack_w_in(w_in):
    d, d_in = w_in.shape
    wt = w_in.T
    n_lat = L_KR + ROPE
    wbig = pl.pallas_call(
        _pack_big_kernel,
        grid=(D_BIG // PACK_COLS,),
        in_specs=[pl.BlockSpec((pl.Element(PACK_COLS), pl.Element(d)),
                               lambda j: (pl.multiple_of(n_lat + j * PACK_COLS, ROPE), 0))],
        out_specs=pl.BlockSpec((d, PACK_COLS), lambda j: (0, j)),
        out_shape=jax.ShapeDtypeStruct((d, D_BIG), jnp.bfloat16),
        compiler_params=pltpu.CompilerParams(dimension_semantics=("parallel",)),
        name="pack_w_big",
    )(wt)
    wlat = pl.pallas_call(
        _pack_lat_kernel,
        grid=(1,),
        in_specs=[pl.BlockSpec((n_lat, d), lambda j: (0, 0))],
        out_specs=pl.BlockSpec((d, D_LAT), lambda j: (0, 0)),
        out_shape=jax.ShapeDtypeStruct((d, D_LAT), jnp.bfloat16),
        name="pack_w_lat",
    )(wt)
    return wlat, wbig


def _pack_weights(w_in, w_uq, w_ukv):
    bf = jnp.bfloat16
    wlat, wbig = _pack_w_in(w_in)
    wq = w_uq.reshape(Q_RANK, HEADS, NOPE + ROPE)
    wuq = jnp.concatenate([wq, _rot_half_cols(wq[..., NOPE:])],
                          axis=-1).reshape(Q_RANK, HEADS * DQK).astype(bf)
    wkv = w_ukv.reshape(KV_RANK, HEADS, NOPE + DV)
    wkn = jnp.concatenate([wkv[..., :NOPE], jnp.zeros((KV_RANK, HEADS, DQK - NOPE), w_ukv.dtype)],
                          axis=-1).reshape(KV_RANK, HEADS * DQK).astype(bf)
    wv = wkv[..., NOPE:].reshape(KV_RANK, HEADS * DV).astype(bf)
    return wlat, wbig, wuq, wkn, wv


def kernel(x, norm_in_g, w_in, q_norm_g, w_uq, kv_norm_g, w_ukv, gmlp_ln_g, gmlp_ln_b,
           w_spatial, b_spatial, out_norm_mla_g, out_norm_gmlp_g, w_out, final_norm_g):
    batch, seq, d = x.shape
    n = batch * seq
    x2 = x.reshape(n, d)
    row = lambda a: a.reshape(1, -1).astype(jnp.float32)

    wlat, wbig, wuq, wkn, wv = _pack_weights(w_in, w_uq, w_ukv)
    ws = w_spatial.astype(jnp.float32)
    bs = jnp.broadcast_to(b_spatial[:, :, None], (GROUPS, SBLOCK, GDIM)).astype(jnp.float32)
    consts = (row(norm_in_g), wlat, wbig, row(q_norm_g), wuq, row(kv_norm_g), wkn, wv,
              row(gmlp_ln_g), row(gmlp_ln_b), ws, bs, row(out_norm_gmlp_g))

    qt, k, vt, gate, og = _proj_call(x2, _rope_tables(seq), consts, batch, seq)
    o_mla = _attn_call(qt, k, vt, batch, seq)
    out = _out_call(x2, o_mla, gate, og, row(out_norm_mla_g), w_out.astype(jnp.bfloat16),
                    row(final_norm_g))
    return out.reshape(batch, seq, d)
```

```python
import math

import jax
import jax.numpy as jnp
from jax import lax
from jax.experimental import pallas as pl
from jax.experimental.pallas import tpu as pltpu

D_MODEL = 1024
CHUNK = 64
EPS = 1e-6

HEADS = 8
NOPE = 64
ROPE = 32
DV = 128
DQK = 128
Q_RANK = 384
KV_RANK = 256
MLA_W = HEADS * DV
ROPE_THETA = 10000.0
QK_SCALE = math.log2(math.e) / math.sqrt(NOPE + ROPE)

GROUPS = 8
GDIM = 128
GMLP_W = GROUPS * GDIM
SBLOCK = 128

L_QLAT = 0
L_KVLAT = L_QLAT + Q_RANK
L_KR = L_KVLAT + KV_RANK
D_LAT = L_KR + DQK
B_GMLA = 0
B_U = B_GMLA + MLA_W
B_V = B_U + GMLP_W
B_GG = B_V + GMLP_W
D_BIG = B_GG + GMLP_W

VMEM_LIMIT = 56 * 1024 * 1024

TM_PROJ = 512
SUB_PROJ = 256
TQ = 512
TK = 512
HB = 8
LB = 128
TM_OUT = 1024
PACK_COLS = 512
POS_SPLIT = 128
SUB_OUT = 256

NEG_BIG = -1e30


def _rms(x, g):
    return x * lax.rsqrt(jnp.mean(x * x, axis=-1, keepdims=True) + EPS) * g


def _dot(a, b):
    return jnp.dot(a, b, preferred_element_type=jnp.float32)


def _silu(x):
    return x * (1.0 / (1.0 + jnp.exp(-x)))


def _gelu(x):
    return 0.5 * x * (1.0 + lax.erf(x * (1.0 / math.sqrt(2.0))))


def _rope(x, cos, sin):
    return x * cos + pltpu.roll(x, DQK - ROPE, 1) * sin


def _proj_kernel(x_ref, cos_ref, sin_ref, ng_ref, wlat_ref, wbig_ref,
                 qg_ref, wuq_ref, kvg_ref, wkn_ref, wv_ref, lng_ref, lnb_ref, ws_ref, bs_ref,
                 ogg_ref, qt_ref, k_ref, vt_ref, gate_ref, og_ref):
    tc = lax.broadcasted_iota(jnp.int32, (SBLOCK, SBLOCK), 0) // CHUNK
    sc = lax.broadcasted_iota(jnp.int32, (SBLOCK, SBLOCK), 1) // CHUNK
    ws = [jnp.where(sc <= tc, ws_ref[g], 0.0).astype(jnp.bfloat16) for g in range(GROUPS)]
    nblk = SUB_PROJ // SBLOCK

    for r0 in range(0, TM_PROJ, SUB_PROJ):
        rows = slice(r0, r0 + SUB_PROJ)
        y = _rms(x_ref[rows, :], ng_ref[...]).astype(jnp.bfloat16)

        v = _gelu(_dot(y, wbig_ref[:, B_V:B_GG]))
        u = _gelu(_dot(y, wbig_ref[:, B_U:B_V]))
        mu = jnp.mean(v, axis=-1, keepdims=True)
        vc = v - mu
        var = jnp.mean(vc * vc, axis=-1, keepdims=True)
        vln = (vc * lax.rsqrt(var + EPS) * lng_ref[...] + lnb_ref[...]).astype(jnp.bfloat16)

        lat = _dot(y, wlat_ref[...])
        gate_ref[rows, :] = _silu(_dot(y, wbig_ref[:, B_GMLA:B_U])).astype(jnp.bfloat16)
        ggs = _silu(_dot(y, wbig_ref[:, B_GG:D_BIG]))

        mixed_g = []
        for g in range(GROUPS):
            cols = slice(g * GDIM, (g + 1) * GDIM)
            vb = jnp.concatenate([vln[c * SBLOCK:(c + 1) * SBLOCK, cols] for c in range(nblk)],
                                 axis=1)
            mixed_g.append(_dot(ws[g], vb))
        mixed = jnp.concatenate(
            [jnp.concatenate([mixed_g[g][:, c * GDIM:(c + 1) * GDIM] + bs_ref[g]
                              for g in range(GROUPS)], axis=1) for c in range(nblk)], axis=0)
        og_ref[rows, :] = (_rms(u * mixed, ogg_ref[...]) * ggs).astype(jnp.bfloat16)

        qn = _rms(lat[:, L_QLAT:L_KVLAT], qg_ref[...]).astype(jnp.bfloat16)
        kvn = _rms(lat[:, L_KVLAT:L_KR], kvg_ref[...]).astype(jnp.bfloat16)
        cos = cos_ref[rows, :]
        sin = sin_ref[rows, :]
        kr = _rope(lat[:, L_KR:D_LAT], cos, sin)
        q_all = _dot(qn, wuq_ref[...])
        k_all = _dot(kvn, wkn_ref[...])
        v_all = _dot(kvn, wv_ref[...])
        cosq = cos * QK_SCALE
        sinq = sin * QK_SCALE
        for h in range(HEADS):
            sl = slice(h * DQK, (h + 1) * DQK)
            qt_ref[0, h, :, rows] = _rope(q_all[:, sl], cosq, sinq).T.astype(jnp.bfloat16)
            k_ref[0, h, rows, :] = (k_all[:, sl] + kr).astype(jnp.bfloat16)
            vt_ref[0, h, :, rows] = v_all[:, sl].T.astype(jnp.bfloat16)


def _const_spec(shape):
    nd = len(shape)
    return pl.BlockSpec(shape, lambda i, _nd=nd: (0,) * _nd, pipeline_mode=pl.Buffered(1))


def _proj_call(x2, tables, consts, batch, seq):
    n = x2.shape[0]
    tm = TM_PROJ
    tiles_per_seq = seq // tm
    row_spec = pl.BlockSpec((tm, D_MODEL), lambda i: (i, 0))
    tab_spec = pl.BlockSpec((tm, DQK), lambda i: (i % tiles_per_seq, 0))
    head_spec = pl.BlockSpec((1, HEADS, tm, DQK),
                             lambda i: (i // tiles_per_seq, 0, i % tiles_per_seq, 0))
    head_shape = jax.ShapeDtypeStruct((batch, HEADS, seq, DQK), jnp.bfloat16)
    headt_spec = pl.BlockSpec((1, HEADS, DQK, tm),
                              lambda i: (i // tiles_per_seq, 0, 0, i % tiles_per_seq))
    headt_shape = jax.ShapeDtypeStruct((batch, HEADS, DQK, seq), jnp.bfloat16)
    row_shape = jax.ShapeDtypeStruct((n, D_MODEL), jnp.bfloat16)
    return pl.pallas_call(
        _proj_kernel,
        grid=(n // tm,),
        in_specs=[row_spec] + [tab_spec] * len(tables) + [_const_spec(c.shape) for c in consts],
        out_specs=[headt_spec, head_spec, headt_spec, row_spec, row_spec],
        out_shape=[headt_shape, head_shape, headt_shape, row_shape, row_shape],
        compiler_params=pltpu.CompilerParams(
            dimension_semantics=("parallel",), vmem_limit_bytes=VMEM_LIMIT),
        name="proj_gmlp",
    )(x2, *tables, *consts)


def _attn_kernel(qt_ref, qtn_ref, k_ref, vt_ref, o_ref, s_ref, cm_ref, acc_ref):
    qi = pl.program_id(2)
    acc_ref[...] = jnp.zeros_like(acc_ref)

    def scores(q_ref, t, j):
        start = pl.multiple_of(t * TK, TK)
        return _dot(k_ref[0, j, pl.ds(start, TK), :], q_ref[0, j])

    def park(j, st):
        s_ref[j] = st
        cm_ref[j] = jnp.max(st, axis=0, keepdims=True)

    def consume(t, j, st, state, cmax):
        m_prev, l_prev = state
        start = pl.multiple_of(t * TK, TK)
        m_new = jnp.maximum(m_prev, cmax)
        alpha = jnp.exp2(m_prev - m_new)
        pt = jnp.exp2(st - m_new)
        l_new = alpha * l_prev + jnp.sum(pt, axis=0, keepdims=True)
        vt = vt_ref[0, j, :, pl.ds(start, TK)]
        return m_new, l_new, alpha, _dot(vt, pt.astype(jnp.bfloat16))

    @pl.when(qi == 0)
    def _():
        for j in range(HB):
            park(j, scores(qt_ref, 0, j))

    def body(t, carry):
        out = []
        for j in range(HB):
            m_prev, l_prev = carry[j]
            m_new, l_new, alpha, pv = consume(t, j, s_ref[j], (m_prev, l_prev), cm_ref[j])
            park(j, scores(qt_ref, t + 1, j))
            acc_ref[j] = alpha * acc_ref[j] + pv
            out.append((m_new, l_new))
        return tuple(out)

    init = tuple((jnp.full((1, TQ), NEG_BIG, jnp.float32), jnp.zeros((1, TQ), jnp.float32))
                 for _ in range(HB))
    def tiles(first, count, c):
        for i in range(count):
            c = body(first + i, c)
        return c

    carry = lax.fori_loop(0, qi // 4, lambda u, c: tiles(4 * u, 4, c), init)
    carry = lax.cond(qi % 4 >= 2, lambda c: tiles((qi // 4) * 4, 2, c), lambda c: c, carry)
    carry = lax.cond(qi % 2 == 1, lambda c: body(qi - 1, c), lambda c: c, carry)

    kc = lax.broadcasted_iota(jnp.int32, (LB, LB), 0) // CHUNK
    qc = lax.broadcasted_iota(jnp.int32, (LB, LB), 1) // CHUNK
    diag_ok = kc <= qc
    start = pl.multiple_of(qi * TK, TK)
    for j in range(HB):
        m_prev, l_prev = carry[j]
        p_blocks, m_blocks, l_blocks = [], [], []
        for c in range(TQ // LB):
            lanes = slice(c * LB, (c + 1) * LB)
            sq = jnp.where(diag_ok, s_ref[j, c * LB:(c + 1) * LB, lanes], NEG_BIG)
            if c:
                st = jnp.concatenate([s_ref[j, 0:c * LB, lanes], sq], axis=0)
            else:
                st = sq
            m_new = jnp.maximum(m_prev[:, lanes], jnp.max(st, axis=0, keepdims=True))
            pt = jnp.exp2(st - m_new)
            m_blocks.append(m_new)
            l_blocks.append(jnp.sum(pt, axis=0, keepdims=True))
            pt = pt.astype(jnp.bfloat16)
            if (c + 1) * LB < TK:
                pt = jnp.concatenate(
                    [pt, jnp.zeros((TK - (c + 1) * LB, LB), jnp.bfloat16)], axis=0)
            p_blocks.append(pt)
        m_new = jnp.concatenate(m_blocks, axis=1)
        alpha = jnp.exp2(m_prev - m_new)
        l_new = alpha * l_prev + jnp.concatenate(l_blocks, axis=1)
        pv = _dot(vt_ref[0, j, :, pl.ds(start, TK)], jnp.concatenate(p_blocks, axis=1))
        park(j, scores(qtn_ref, 0, j))
        ot = (alpha * acc_ref[j] + pv) * (1.0 / l_new)
        o_ref[:, j * DV:(j + 1) * DV] = ot.T.astype(o_ref.dtype)


def _attn_call(qt, k, vt, batch, seq):
    assert TQ == TK
    nq = seq // TQ
    return pl.pallas_call(
        _attn_kernel,
        grid=(batch, HEADS // HB, nq),
        in_specs=[
            pl.BlockSpec((1, HB, DQK, TQ), lambda b, h, i: (b, h, 0, i)),
            pl.BlockSpec((1, HB, DQK, TQ), lambda b, h, i: (b, h, 0, jnp.minimum(i + 1, nq - 1))),
            pl.BlockSpec((1, HB, seq, DQK), lambda b, h, i: (b, h, 0, 0),
                         pipeline_mode=pl.Buffered(1)),
            pl.BlockSpec((1, HB, DV, seq), lambda b, h, i: (b, h, 0, 0),
                         pipeline_mode=pl.Buffered(1)),
        ],
        out_specs=pl.BlockSpec((TQ, HB * DV), lambda b, h, i: (b * nq + i, h)),
        out_shape=jax.ShapeDtypeStruct((batch * seq, MLA_W), jnp.bfloat16),
        scratch_shapes=[pltpu.VMEM((HB, TK, TQ), jnp.float32),
                        pltpu.VMEM((HB, 1, TQ), jnp.float32),
                        pltpu.VMEM((HB, DV, TQ), jnp.float32)],
        compiler_params=pltpu.CompilerParams(
            dimension_semantics=("parallel", "parallel", "arbitrary"),
            vmem_limit_bytes=VMEM_LIMIT),
        name="mla_attn",
    )(qt, qt, k, vt)


def _out_kernel(x_ref, o_ref, gate_ref, og_ref, omg_ref, wo_ref, fg_ref, out_ref):
    for r0 in range(0, TM_OUT, SUB_OUT):
        rows = slice(r0, r0 + SUB_OUT)
        h = x_ref[rows, :] + _dot(og_ref[rows, :], wo_ref[MLA_W:, :])
        om = _rms(o_ref[rows, :].astype(jnp.float32), omg_ref[...]) \
            * gate_ref[rows, :].astype(jnp.float32)
        h = h + _dot(om.astype(jnp.bfloat16), wo_ref[0:MLA_W, :])
        out_ref[rows, :] = _rms(h, fg_ref[...])


def _out_call(x2, o_mla, gate, og, omg, wo, fg):
    n = x2.shape[0]
    tm = TM_OUT
    row_spec = pl.BlockSpec((tm, D_MODEL), lambda i: (i, 0))
    return pl.pallas_call(
        _out_kernel,
        grid=(n // tm,),
        in_specs=[row_spec, row_spec, row_spec, row_spec,
                  _const_spec(omg.shape), _const_spec(wo.shape), _const_spec(fg.shape)],
        out_specs=row_spec,
        out_shape=jax.ShapeDtypeStruct((n, D_MODEL), jnp.float32),
        compiler_params=pltpu.CompilerParams(
            dimension_semantics=("parallel",), vmem_limit_bytes=VMEM_LIMIT),
        name="out_proj",
    )(x2, o_mla, gate, og, omg, wo, fg)


def _rope_tables(seq):
    inv_freq = ROPE_THETA ** (-jnp.arange(0, ROPE, 2, dtype=jnp.float32) / ROPE)
    hi = jnp.arange(0, seq, POS_SPLIT, dtype=jnp.float32)[:, None] * inv_freq[None, :]
    lo = jnp.arange(POS_SPLIT, dtype=jnp.float32)[:, None] * inv_freq[None, :]
    ch, sh = jnp.cos(hi)[:, None, :], jnp.sin(hi)[:, None, :]
    cl, sl = jnp.cos(lo)[None, :, :], jnp.sin(lo)[None, :, :]
    cos = (ch * cl - sh * sl).reshape(seq, ROPE // 2)
    sin = (sh * cl + ch * sl).reshape(seq, ROPE // 2)
    ones = jnp.ones((seq, NOPE), jnp.float32)
    zeros_n = jnp.zeros((seq, NOPE), jnp.float32)
    pad = jnp.zeros((seq, DQK - NOPE - ROPE), jnp.float32)
    cos_t = jnp.concatenate([ones, cos, cos, pad], axis=1)
    sin_t = jnp.concatenate([zeros_n, sin, sin, pad], axis=1)
    return cos_t, sin_t


def _rot_half_cols(w):
    half = ROPE // 2
    return jnp.concatenate([-w[..., half:], w[..., :half]], axis=-1)


def _pack_big_kernel(wt_ref, wbig_ref):
    wbig_ref[...] = wt_ref[...].T.astype(jnp.bfloat16)


def _pack_lat_kernel(wt_ref, wlat_ref):
    w = wt_ref[...]
    kr = w[L_KR:L_KR + ROPE]
    half = ROPE // 2
    blk = jnp.concatenate([w[:L_KR], jnp.zeros((NOPE, w.shape[1]), w.dtype), kr,
                           -kr[half:], kr[:half]], axis=0)
    wlat_ref[...] = blk.T.astype(jnp.bfloat16)


def _pack_w_in(w_in):
    d, d_in = w_in.shape
    wt = w_in.T
    n_lat = L_KR + ROPE
    wbig = pl.pallas_call(
        _pack_big_kernel,
        grid=(D_BIG // PACK_COLS,),
        in_specs=[pl.BlockSpec((pl.Element(PACK_COLS), pl.Element(d)),
                               lambda j: (pl.multiple_of(n_lat + j * PACK_COLS, ROPE), 0))],
        out_specs=pl.BlockSpec((d, PACK_COLS), lambda j: (0, j)),
        out_shape=jax.ShapeDtypeStruct((d, D_BIG), jnp.bfloat16),
        compiler_params=pltpu.CompilerParams(dimension_semantics=("parallel",)),
        name="pack_w_big",
    )(wt)
    wlat = pl.pallas_call(
        _pack_lat_kernel,
        grid=(1,),
        in_specs=[pl.BlockSpec((n_lat, d), lambda j: (0, 0))],
        out_specs=pl.BlockSpec((d, D_LAT), lambda j: (0, 0)),
        out_shape=jax.ShapeDtypeStruct((d, D_LAT), jnp.bfloat16),
        name="pack_w_lat",
    )(wt)
    return wlat, wbig


def _pack_weights(w_in, w_uq, w_ukv):
    bf = jnp.bfloat16
    wlat, wbig = _pack_w_in(w_in)
    wq = w_uq.reshape(Q_RANK, HEADS, NOPE + ROPE)
    wuq = jnp.concatenate([wq, _rot_half_cols(wq[..., NOPE:])],
                          axis=-1).reshape(Q_RANK, HEADS * DQK).astype(bf)
    wkv = w_ukv.reshape(KV_RANK, HEADS, NOPE + DV)
    wkn = jnp.concatenate([wkv[..., :NOPE], jnp.zeros((KV_RANK, HEADS, DQK - NOPE), w_ukv.dtype)],
                          axis=-1).reshape(KV_RANK, HEADS * DQK).astype(bf)
    wv = wkv[..., NOPE:].reshape(KV_RANK, HEADS * DV).astype(bf)
    return wlat, wbig, wuq, wkn, wv


def kernel(x, norm_in_g, w_in, q_norm_g, w_uq, kv_norm_g, w_ukv, gmlp_ln_g, gmlp_ln_b,
           w_spatial, b_spatial, out_norm_mla_g, out_norm_gmlp_g, w_out, final_norm_g):
    batch, seq, d = x.shape
    n = batch * seq
    x2 = x.reshape(n, d)
    row = lambda a: a.reshape(1, -1).astype(jnp.float32)

    wlat, wbig, wuq, wkn, wv = _pack_weights(w_in, w_uq, w_ukv)
    ws = w_spatial.astype(jnp.float32)
    bs = jnp.broadcast_to(b_spatial[:, :, None], (GROUPS, SBLOCK, GDIM)).astype(jnp.float32)
    consts = (row(norm_in_g), wlat, wbig, row(q_norm_g), wuq, row(kv_norm_g), wkn, wv,
              row(gmlp_ln_g), row(gmlp_ln_b), ws, bs, row(out_norm_gmlp_g))

    qt, k, vt, gate, og = _proj_call(x2, _rope_tables(seq), consts, batch, seq)
    o_mla = _attn_call(qt, k, vt, batch, seq)
    out = _out_call(x2, o_mla, gate, og, row(out_norm_mla_g), w_out.astype(jnp.bfloat16),
                    row(final_norm_g))
    return out.reshape(batch, seq, d)
```

```python
import functools
import math

import jax
import jax.numpy as jnp
from jax import lax
from jax.experimental import pallas as pl
from jax.experimental.pallas import tpu as pltpu

D_MODEL = 1024
CHUNK = 64
EPS = 1e-6

HEADS = 8
NOPE = 64
ROPE = 32
DV = 128
DQK = 128
Q_RANK = 384
KV_RANK = 256
MLA_W = HEADS * DV
ROPE_THETA = 10000.0
QK_SCALE = math.log2(math.e) / math.sqrt(NOPE + ROPE)

GROUPS = 8
GDIM = 128
GMLP_W = GROUPS * GDIM
SBLOCK = 128

L_QLAT = 0
L_KVLAT = L_QLAT + Q_RANK
L_KR = L_KVLAT + KV_RANK
D_LAT = L_KR + DQK
B_GMLA = 0
B_U = B_GMLA + MLA_W
B_V = B_U + GMLP_W
B_GG = B_V + GMLP_W
D_BIG = B_GG + GMLP_W

VMEM_LIMIT = 56 * 1024 * 1024

TM_PROJ = 512
SUB_PROJ = 256
TQ = 512
TK = 512
HB = 8
KV_CHUNKS = 4
LB = 128
TM_OUT = 1024
PACK_COLS = 512
POS_SPLIT = 128
SUB_OUT = 256

NEG_BIG = -1e30


def _rms(x, g):
    return x * lax.rsqrt(jnp.mean(x * x, axis=-1, keepdims=True) + EPS) * g


def _dot(a, b):
    return jnp.dot(a, b, preferred_element_type=jnp.float32)


def _silu(x):
    return x * (1.0 / (1.0 + jnp.exp(-x)))


def _gelu(x):
    return 0.5 * x * (1.0 + lax.erf(x * (1.0 / math.sqrt(2.0))))


def _rope(x, cos, sin):
    return x * cos + pltpu.roll(x, DQK - ROPE, 1) * sin


def _proj_kernel(x_ref, cos_ref, sin_ref, ng_ref, wlat_ref, wbig_ref,
                 qg_ref, wuq_ref, kvg_ref, wkn_ref, wv_ref, lng_ref, lnb_ref, ws_ref, bs_ref,
                 ogg_ref, qt_ref, k_ref, vt_ref, gate_ref, og_ref):
    tc = lax.broadcasted_iota(jnp.int32, (SBLOCK, SBLOCK), 0) // CHUNK
    sc = lax.broadcasted_iota(jnp.int32, (SBLOCK, SBLOCK), 1) // CHUNK
    ws = [jnp.where(sc <= tc, ws_ref[g], 0.0).astype(jnp.bfloat16) for g in range(GROUPS)]
    nblk = SUB_PROJ // SBLOCK

    for r0 in range(0, TM_PROJ, SUB_PROJ):
        rows = slice(r0, r0 + SUB_PROJ)
        y = _rms(x_ref[rows, :], ng_ref[...]).astype(jnp.bfloat16)

        v = _gelu(_dot(y, wbig_ref[:, B_V:B_GG]))
        u = _gelu(_dot(y, wbig_ref[:, B_U:B_V]))
        mu = jnp.mean(v, axis=-1, keepdims=True)
        vc = v - mu
        var = jnp.mean(vc * vc, axis=-1, keepdims=True)
        vln = (vc * lax.rsqrt(var + EPS) * lng_ref[...] + lnb_ref[...]).astype(jnp.bfloat16)

        lat = _dot(y, wlat_ref[...])
        gate_ref[rows, :] = _silu(_dot(y, wbig_ref[:, B_GMLA:B_U])).astype(jnp.bfloat16)
        ggs = _silu(_dot(y, wbig_ref[:, B_GG:D_BIG]))

        mixed_g = []
        for g in range(GROUPS):
            cols = slice(g * GDIM, (g + 1) * GDIM)
            vb = jnp.concatenate([vln[c * SBLOCK:(c + 1) * SBLOCK, cols] for c in range(nblk)],
                                 axis=1)
            mixed_g.append(_dot(ws[g], vb))
        mixed = jnp.concatenate(
            [jnp.concatenate([mixed_g[g][:, c * GDIM:(c + 1) * GDIM] + bs_ref[g]
                              for g in range(GROUPS)], axis=1) for c in range(nblk)], axis=0)
        og_ref[rows, :] = (_rms(u * mixed, ogg_ref[...]) * ggs).astype(jnp.bfloat16)

        qn = _rms(lat[:, L_QLAT:L_KVLAT], qg_ref[...]).astype(jnp.bfloat16)
        kvn = _rms(lat[:, L_KVLAT:L_KR], kvg_ref[...]).astype(jnp.bfloat16)
        cos = cos_ref[rows, :]
        sin = sin_ref[rows, :]
        kr = _rope(lat[:, L_KR:D_LAT], cos, sin)
        q_all = _dot(qn, wuq_ref[...])
        k_all = _dot(kvn, wkn_ref[...])
        v_all = _dot(kvn, wv_ref[...])
        cosq = cos * QK_SCALE
        sinq = sin * QK_SCALE
        for h in range(HEADS):
            sl = slice(h * DQK, (h + 1) * DQK)
            qt_ref[0, h, :, rows] = _rope(q_all[:, sl], cosq, sinq).T.astype(jnp.bfloat16)
            k_ref[0, h, rows, :] = (k_all[:, sl] + kr).astype(jnp.bfloat16)
            vt_ref[0, h, :, rows] = v_all[:, sl].T.astype(jnp.bfloat16)


def _const_spec(shape):
    nd = len(shape)
    return pl.BlockSpec(shape, lambda i, _nd=nd: (0,) * _nd, pipeline_mode=pl.Buffered(1))


def _proj_call(x2, tables, consts, batch, seq):
    n = x2.shape[0]
    tm = TM_PROJ
    tiles_per_seq = seq // tm
    row_spec = pl.BlockSpec((tm, D_MODEL), lambda i: (i, 0))
    tab_spec = pl.BlockSpec((tm, DQK), lambda i: (i % tiles_per_seq, 0))
    head_spec = pl.BlockSpec((1, HEADS, tm, DQK),
                             lambda i: (i // tiles_per_seq, 0, i % tiles_per_seq, 0))
    head_shape = jax.ShapeDtypeStruct((batch, HEADS, seq, DQK), jnp.bfloat16)
    headt_spec = pl.BlockSpec((1, HEADS, DQK, tm),
                              lambda i: (i // tiles_per_seq, 0, 0, i % tiles_per_seq))
    headt_shape = jax.ShapeDtypeStruct((batch, HEADS, DQK, seq), jnp.bfloat16)
    row_shape = jax.ShapeDtypeStruct((n, D_MODEL), jnp.bfloat16)
    return pl.pallas_call(
        _proj_kernel,
        grid=(n // tm,),
        in_specs=[row_spec] + [tab_spec] * len(tables) + [_const_spec(c.shape) for c in consts],
        out_specs=[headt_spec, head_spec, headt_spec, row_spec, row_spec],
        out_shape=[headt_shape, head_shape, headt_shape, row_shape, row_shape],
        compiler_params=pltpu.CompilerParams(
            dimension_semantics=("parallel",), vmem_limit_bytes=VMEM_LIMIT),
        name="proj_gmlp",
    )(x2, *tables, *consts)


def _attn_kernel(qt_ref, qtn_ref, k_hbm, vt_hbm, o_ref, k_ref, vt_ref, ksem, vsem,
                 s_ref, cm_ref, acc_ref, *, nb, nq):
    b = pl.program_id(0)
    qi = pl.program_id(1)
    chunk = k_ref.shape[1] // KV_CHUNKS
    assert chunk == 4 * TK and (nq - 1) // 4 >= 1
    more_batches = b < nb - 1
    acc_ref[...] = jnp.zeros_like(acc_ref)

    def k_copy(bb, c):
        rows = pl.ds(c * chunk, chunk)
        return pltpu.make_async_copy(k_hbm.at[bb, :, rows, :], k_ref.at[:, rows, :], ksem.at[c])

    def v_copy(bb, c):
        cols = pl.ds(c * chunk, chunk)
        return pltpu.make_async_copy(vt_hbm.at[bb, :, :, cols], vt_ref.at[:, :, cols],
                                     vsem.at[c])

    @pl.when(qi == 0)
    def _():
        @pl.when(b == 0)
        def _():
            k_copy(0, 0).start()
            v_copy(0, 0).start()
            k_copy(0, 0).wait()
        v_copy(b, 0).wait()
        for c in range(1, KV_CHUNKS):
            k_copy(b, c).start()
            v_copy(b, c).start()

    for c in range(1, KV_CHUNKS):
        @pl.when(qi == c * (chunk // TK))
        def _(c=c):
            k_copy(b, c).wait()
            v_copy(b, c).wait()

    def scores(q_ref, t, j):
        start = pl.multiple_of(t * TK, TK)
        return _dot(k_ref[j, pl.ds(start, TK), :], q_ref[0, j])

    def park(j, st):
        s_ref[j] = st
        cm_ref[j] = jnp.max(st, axis=0, keepdims=True)

    def consume(t, j, st, state, cmax):
        m_prev, l_prev = state
        start = pl.multiple_of(t * TK, TK)
        m_new = jnp.maximum(m_prev, cmax)
        alpha = jnp.exp2(m_prev - m_new)
        pt = jnp.exp2(st - m_new)
        l_new = alpha * l_prev + jnp.sum(pt, axis=0, keepdims=True)
        vt = vt_ref[j, :, pl.ds(start, TK)]
        return m_new, l_new, alpha, _dot(vt, pt.astype(jnp.bfloat16))

    @pl.when((qi == 0) & (b == 0))
    def _():
        for j in range(HB):
            park(j, scores(qt_ref, 0, j))

    def body(t, carry):
        out = []
        for j in range(HB):
            m_prev, l_prev = carry[j]
            m_new, l_new, alpha, pv = consume(t, j, s_ref[j], (m_prev, l_prev), cm_ref[j])
            park(j, scores(qt_ref, t + 1, j))
            acc_ref[j] = alpha * acc_ref[j] + pv
            out.append((m_new, l_new))
        return tuple(out)

    init = tuple((jnp.full((1, TQ), NEG_BIG, jnp.float32), jnp.zeros((1, TQ), jnp.float32))
                 for _ in range(HB))
    def tiles(first, count, c):
        for i in range(count):
            c = body(first + i, c)
        return c

    carry = lax.fori_loop(0, qi // 4, lambda u, c: tiles(4 * u, 4, c), init)

    @pl.when((qi == nq - 1) & more_batches)
    def _():
        k_copy(b + 1, 0).start()
        v_copy(b + 1, 0).start()

    carry = lax.cond(qi % 4 >= 2, lambda c: tiles((qi // 4) * 4, 2, c), lambda c: c, carry)
    carry = lax.cond(qi % 2 == 1, lambda c: body(qi - 1, c), lambda c: c, carry)

    @pl.when((qi == nq - 1) & more_batches)
    def _():
        k_copy(b + 1, 0).wait()

    kc = lax.broadcasted_iota(jnp.int32, (LB, LB), 0) // CHUNK
    qc = lax.broadcasted_iota(jnp.int32, (LB, LB), 1) // CHUNK
    diag_ok = kc <= qc
    start = pl.multiple_of(qi * TK, TK)
    for j in range(HB):
        m_prev, l_prev = carry[j]
        p_blocks, m_blocks, l_blocks = [], [], []
        for c in range(TQ // LB):
            lanes = slice(c * LB, (c + 1) * LB)
            sq = jnp.where(diag_ok, s_ref[j, c * LB:(c + 1) * LB, lanes], NEG_BIG)
            if c:
                st = jnp.concatenate([s_ref[j, 0:c * LB, lanes], sq], axis=0)
            else:
                st = sq
            m_new = jnp.maximum(m_prev[:, lanes], jnp.max(st, axis=0, keepdims=True))
            pt = jnp.exp2(st - m_new)
            m_blocks.append(m_new)
            l_blocks.append(jnp.sum(pt, axis=0, keepdims=True))
            pt = pt.astype(jnp.bfloat16)
            if (c + 1) * LB < TK:
                pt = jnp.concatenate(
                    [pt, jnp.zeros((TK - (c + 1) * LB, LB), jnp.bfloat16)], axis=0)
            p_blocks.append(pt)
        m_new = jnp.concatenate(m_blocks, axis=1)
        alpha = jnp.exp2(m_prev - m_new)
        l_new = alpha * l_prev + jnp.concatenate(l_blocks, axis=1)
        pv = _dot(vt_ref[j, :, pl.ds(start, TK)], jnp.concatenate(p_blocks, axis=1))
        park(j, scores(qtn_ref, 0, j))
        ot = (alpha * acc_ref[j] + pv) * (1.0 / l_new)
        o_ref[:, j * DV:(j + 1) * DV] = ot.T.astype(o_ref.dtype)


def _attn_call(qt, k, vt, batch, seq):
    assert TQ == TK and HB == HEADS
    nq = seq // TQ

    def next_tile(b, i):
        last = i == nq - 1
        return (jnp.where(last, jnp.minimum(b + 1, batch - 1), b), 0, 0,
                jnp.where(last, 0, i + 1))

    return pl.pallas_call(
        functools.partial(_attn_kernel, nb=batch, nq=nq),
        grid=(batch, nq),
        in_specs=[
            pl.BlockSpec((1, HB, DQK, TQ), lambda b, i: (b, 0, 0, i)),
            pl.BlockSpec((1, HB, DQK, TQ), next_tile),
            pl.BlockSpec(memory_space=pl.ANY),
            pl.BlockSpec(memory_space=pl.ANY),
        ],
        out_specs=pl.BlockSpec((TQ, HB * DV), lambda b, i: (b * nq + i, 0)),
        out_shape=jax.ShapeDtypeStruct((batch * seq, MLA_W), jnp.bfloat16),
        scratch_shapes=[pltpu.VMEM((HB, seq, DQK), jnp.bfloat16),
                        pltpu.VMEM((HB, DV, seq), jnp.bfloat16),
                        pltpu.SemaphoreType.DMA((KV_CHUNKS,)),
                        pltpu.SemaphoreType.DMA((KV_CHUNKS,)),
                        pltpu.VMEM((HB, TK, TQ), jnp.float32),
                        pltpu.VMEM((HB, 1, TQ), jnp.float32),
                        pltpu.VMEM((HB, DV, TQ), jnp.float32)],
        compiler_params=pltpu.CompilerParams(
            dimension_semantics=("arbitrary", "arbitrary"),
            vmem_limit_bytes=VMEM_LIMIT),
        name="mla_attn",
    )(qt, qt, k, vt)


def _out_kernel(x_ref, o_ref, gate_ref, og_ref, omg_ref, wo_ref, fg_ref, out_ref):
    for r0 in range(0, TM_OUT, SUB_OUT):
        rows = slice(r0, r0 + SUB_OUT)
        h = x_ref[rows, :] + _dot(og_ref[rows, :], wo_ref[MLA_W:, :])
        om = _rms(o_ref[rows, :].astype(jnp.float32), omg_ref[...]) \
            * gate_ref[rows, :].astype(jnp.float32)
        h = h + _dot(om.astype(jnp.bfloat16), wo_ref[0:MLA_W, :])
        out_ref[rows, :] = _rms(h, fg_ref[...])


def _out_call(x2, o_mla, gate, og, omg, wo, fg):
    n = x2.shape[0]
    tm = TM_OUT
    row_spec = pl.BlockSpec((tm, D_MODEL), lambda i: (i, 0))
    return pl.pallas_call(
        _out_kernel,
        grid=(n // tm,),
        in_specs=[row_spec, row_spec, row_spec, row_spec,
                  _const_spec(omg.shape), _const_spec(wo.shape), _const_spec(fg.shape)],
        out_specs=row_spec,
        out_shape=jax.ShapeDtypeStruct((n, D_MODEL), jnp.float32),
        compiler_params=pltpu.CompilerParams(
            dimension_semantics=("parallel",), vmem_limit_bytes=VMEM_LIMIT),
        name="out_proj",
    )(x2, o_mla, gate, og, omg, wo, fg)


def _rope_tables(seq):
    inv_freq = ROPE_THETA ** (-jnp.arange(0, ROPE, 2, dtype=jnp.float32) / ROPE)
    hi = jnp.arange(0, seq, POS_SPLIT, dtype=jnp.float32)[:, None] * inv_freq[None, :]
    lo = jnp.arange(POS_SPLIT, dtype=jnp.float32)[:, None] * inv_freq[None, :]
    ch, sh = jnp.cos(hi)[:, None, :], jnp.sin(hi)[:, None, :]
    cl, sl = jnp.cos(lo)[None, :, :], jnp.sin(lo)[None, :, :]
    cos = (ch * cl - sh * sl).reshape(seq, ROPE // 2)
    sin = (sh * cl + ch * sl).reshape(seq, ROPE // 2)
    ones = jnp.ones((seq, NOPE), jnp.float32)
    zeros_n = jnp.zeros((seq, NOPE), jnp.float32)
    pad = jnp.zeros((seq, DQK - NOPE - ROPE), jnp.float32)
    cos_t = jnp.concatenate([ones, cos, cos, pad], axis=1)
    sin_t = jnp.concatenate([zeros_n, sin, sin, pad], axis=1)
    return cos_t, sin_t


def _rot_half_cols(w):
    half = ROPE // 2
    return jnp.concatenate([-w[..., half:], w[..., :half]], axis=-1)


def _pack_big_kernel(wt_ref, wbig_ref):
    wbig_ref[...] = wt_ref[...].T.astype(jnp.bfloat16)


def _pack_lat_kernel(wt_ref, wlat_ref):
    w = wt_ref[...]
    kr = w[L_KR:L_KR + ROPE]
    half = ROPE // 2
    blk = jnp.concatenate([w[:L_KR], jnp.zeros((NOPE, w.shape[1]), w.dtype), kr,
                           -kr[half:], kr[:half]], axis=0)
    wlat_ref[...] = blk.T.astype(jnp.bfloat16)


def _pack_w_in(w_in):
    d, d_in = w_in.shape
    wt = w_in.T
    n_lat = L_KR + ROPE
    wbig = pl.pallas_call(
        _pack_big_kernel,
        grid=(D_BIG // PACK_COLS,),
        in_specs=[pl.BlockSpec((pl.Element(PACK_COLS), pl.Element(d)),
                               lambda j: (pl.multiple_of(n_lat + j * PACK_COLS, ROPE), 0))],
        out_specs=pl.BlockSpec((d, PACK_COLS), lambda j: (0, j)),
        out_shape=jax.ShapeDtypeStruct((d, D_BIG), jnp.bfloat16),
        compiler_params=pltpu.CompilerParams(dimension_semantics=("parallel",)),
        name="pack_w_big",
    )(wt)
    wlat = pl.pallas_call(
        _pack_lat_kernel,
        grid=(1,),
        in_specs=[pl.BlockSpec((n_lat, d), lambda j: (0, 0))],
        out_specs=pl.BlockSpec((d, D_LAT), lambda j: (0, 0)),
        out_shape=jax.ShapeDtypeStruct((d, D_LAT), jnp.bfloat16),
        name="pack_w_lat",
    )(wt)
    return wlat, wbig


def _pack_weights(w_in, w_uq, w_ukv):
    bf = jnp.bfloat16
    wlat, wbig = _pack_w_in(w_in)
    wq = w_uq.reshape(Q_RANK, HEADS, NOPE + ROPE)
    wuq = jnp.concatenate([wq, _rot_half_cols(wq[..., NOPE:])],
                          axis=-1).reshape(Q_RANK, HEADS * DQK).astype(bf)
    wkv = w_ukv.reshape(KV_RANK, HEADS, NOPE + DV)
    wkn = jnp.concatenate([wkv[..., :NOPE], jnp.zeros((KV_RANK, HEADS, DQK - NOPE), w_ukv.dtype)],
                          axis=-1).reshape(KV_RANK, HEADS * DQK).astype(bf)
    wv = wkv[..., NOPE:].reshape(KV_RANK, HEADS * DV).astype(bf)
    return wlat, wbig, wuq, wkn, wv


def kernel(x, norm_in_g, w_in, q_norm_g, w_uq, kv_norm_g, w_ukv, gmlp_ln_g, gmlp_ln_b,
           w_spatial, b_spatial, out_norm_mla_g, out_norm_gmlp_g, w_out, final_norm_g):
    batch, seq, d = x.shape
    n = batch * seq
    x2 = x.reshape(n, d)
    row = lambda a: a.reshape(1, -1).astype(jnp.float32)

    wlat, wbig, wuq, wkn, wv = _pack_weights(w_in, w_uq, w_ukv)
    ws = w_spatial.astype(jnp.float32)
    bs = jnp.broadcast_to(b_spatial[:, :, None], (GROUPS, SBLOCK, GDIM)).astype(jnp.float32)
    consts = (row(norm_in_g), wlat, wbig, row(q_norm_g), wuq, row(kv_norm_g), wkn, wv,
              row(gmlp_ln_g), row(gmlp_ln_b), ws, bs, row(out_norm_gmlp_g))

    qt, k, vt, gate, og = _proj_call(x2, _rope_tables(seq), consts, batch, seq)
    o_mla = _attn_call(qt, k, vt, batch, seq)
    out = _out_call(x2, o_mla, gate, og, row(out_norm_mla_g), w_out.astype(jnp.bfloat16),
                    row(final_norm_g))
    return out.reshape(batch, seq, d)
```

```python
import functools
import math

import jax
import jax.numpy as jnp
from jax import lax
from jax.experimental import pallas as pl
from jax.experimental.pallas import tpu as pltpu

D_MODEL = 1024
CHUNK = 64
EPS = 1e-6

HEADS = 8
NOPE = 64
ROPE = 32
DV = 128
DQK = 128
Q_RANK = 384
KV_RANK = 256
MLA_W = HEADS * DV
ROPE_THETA = 10000.0
QK_SCALE = math.log2(math.e) / math.sqrt(NOPE + ROPE)

GROUPS = 8
GDIM = 128
GMLP_W = GROUPS * GDIM
SBLOCK = 128

L_QLAT = 0
L_KVLAT = L_QLAT + Q_RANK
L_KR = L_KVLAT + KV_RANK
D_LAT = L_KR + DQK
B_GMLA = 0
B_U = B_GMLA + MLA_W
B_V = B_U + GMLP_W
B_GG = B_V + GMLP_W
D_BIG = B_GG + GMLP_W

VMEM_LIMIT = 56 * 1024 * 1024

TM_PROJ = 512
SUB_PROJ = 256
TQ = 512
TK = 512
HB = 8
KV_CHUNKS = 4
LB = 128
TM_OUT = 1024
PACK_COLS = 512
POS_SPLIT = 128
SUB_OUT = 256

NEG_BIG = -1e30


def _rms(x, g):
    return x * lax.rsqrt(jnp.mean(x * x, axis=-1, keepdims=True) + EPS) * g


def _dot(a, b):
    return jnp.dot(a, b, preferred_element_type=jnp.float32)


def _silu(x):
    return x * (1.0 / (1.0 + jnp.exp(-x)))


def _gelu(x):
    return 0.5 * x * (1.0 + lax.erf(x * (1.0 / math.sqrt(2.0))))


def _rope(x, cos, sin):
    return x * cos + pltpu.roll(x, DQK - ROPE, 1) * sin


def _proj_kernel(x_ref, cost_ref, sint_ref, ng_ref, wlat_ref, wbig_ref,
                 qg_ref, wuqt_ref, kvg_ref, wkn_ref, wvt_ref, lng_ref, lnb_ref, ws_ref, bs_ref,
                 ogg_ref, qt_ref, k_ref, vt_ref, gate_ref, og_ref):
    tc = lax.broadcasted_iota(jnp.int32, (SBLOCK, SBLOCK), 0) // CHUNK
    sc = lax.broadcasted_iota(jnp.int32, (SBLOCK, SBLOCK), 1) // CHUNK
    ws = [jnp.where(sc <= tc, ws_ref[g], 0.0).astype(jnp.bfloat16) for g in range(GROUPS)]
    nblk = SUB_PROJ // SBLOCK

    for r0 in range(0, TM_PROJ, SUB_PROJ):
        rows = slice(r0, r0 + SUB_PROJ)
        y = _rms(x_ref[rows, :], ng_ref[...]).astype(jnp.bfloat16)

        v = _gelu(_dot(y, wbig_ref[:, B_V:B_GG]))
        u = _gelu(_dot(y, wbig_ref[:, B_U:B_V]))
        mu = jnp.mean(v, axis=-1, keepdims=True)
        vc = v - mu
        var = jnp.mean(vc * vc, axis=-1, keepdims=True)
        vln = (vc * lax.rsqrt(var + EPS) * lng_ref[...] + lnb_ref[...]).astype(jnp.bfloat16)

        lat = _dot(y, wlat_ref[...])
        gate_ref[rows, :] = _silu(_dot(y, wbig_ref[:, B_GMLA:B_U])).astype(jnp.bfloat16)
        ggs = _silu(_dot(y, wbig_ref[:, B_GG:D_BIG]))

        mixed_g = []
        for g in range(GROUPS):
            cols = slice(g * GDIM, (g + 1) * GDIM)
            vb = jnp.concatenate([vln[c * SBLOCK:(c + 1) * SBLOCK, cols] for c in range(nblk)],
                                 axis=1)
            mixed_g.append(_dot(ws[g], vb))
        mixed = jnp.concatenate(
            [jnp.concatenate([mixed_g[g][:, c * GDIM:(c + 1) * GDIM] + bs_ref[g]
                              for g in range(GROUPS)], axis=1) for c in range(nblk)], axis=0)
        og_ref[rows, :] = (_rms(u * mixed, ogg_ref[...]) * ggs).astype(jnp.bfloat16)

        qn = _rms(lat[:, L_QLAT:L_KVLAT], qg_ref[...])
        kvn = _rms(lat[:, L_KVLAT:L_KR], kvg_ref[...])
        cos_t = cost_ref[:, rows]
        sin_t = sint_ref[:, rows]
        kr = _rope(lat[:, L_KR:D_LAT], cos_t.T, sin_t.T)
        k_all = _dot(kvn.astype(jnp.bfloat16), wkn_ref[...])
        qt_all = _dot(wuqt_ref[...], qn.T.astype(jnp.bfloat16))
        vt_all = _dot(wvt_ref[...], kvn.T.astype(jnp.bfloat16))
        cosq = cos_t * QK_SCALE
        sinq = sin_t * QK_SCALE
        for h in range(HEADS):
            sl = slice(h * DQK, (h + 1) * DQK)
            qb = qt_all[sl, :]
            rot = jnp.concatenate([qb[:NOPE], qb[NOPE + ROPE:], qb[NOPE + ROPE:]], axis=0)
            qt_ref[0, h, :, rows] = (qb * cosq + rot * sinq).astype(jnp.bfloat16)
            k_ref[0, h, rows, :] = (k_all[:, sl] + kr).astype(jnp.bfloat16)
            vt_ref[0, h, :, rows] = vt_all[sl, :].astype(jnp.bfloat16)


def _const_spec(shape):
    nd = len(shape)
    return pl.BlockSpec(shape, lambda i, _nd=nd: (0,) * _nd, pipeline_mode=pl.Buffered(1))


def _proj_call(x2, tables, consts, batch, seq):
    n = x2.shape[0]
    tm = TM_PROJ
    tiles_per_seq = seq // tm
    row_spec = pl.BlockSpec((tm, D_MODEL), lambda i: (i, 0))
    tab_spec = pl.BlockSpec((DQK, tm), lambda i: (0, i % tiles_per_seq))
    head_spec = pl.BlockSpec((1, HEADS, tm, DQK),
                             lambda i: (i // tiles_per_seq, 0, i % tiles_per_seq, 0))
    head_shape = jax.ShapeDtypeStruct((batch, HEADS, seq, DQK), jnp.bfloat16)
    headt_spec = pl.BlockSpec((1, HEADS, DQK, tm),
                              lambda i: (i // tiles_per_seq, 0, 0, i % tiles_per_seq))
    headt_shape = jax.ShapeDtypeStruct((batch, HEADS, DQK, seq), jnp.bfloat16)
    row_shape = jax.ShapeDtypeStruct((n, D_MODEL), jnp.bfloat16)
    return pl.pallas_call(
        _proj_kernel,
        grid=(n // tm,),
        in_specs=[row_spec] + [tab_spec] * len(tables) + [_const_spec(c.shape) for c in consts],
        out_specs=[headt_spec, head_spec, headt_spec, row_spec, row_spec],
        out_shape=[headt_shape, head_shape, headt_shape, row_shape, row_shape],
        compiler_params=pltpu.CompilerParams(
            dimension_semantics=("parallel",), vmem_limit_bytes=VMEM_LIMIT),
        name="proj_gmlp",
    )(x2, *tables, *consts)


def _attn_kernel(qt_ref, qtn_ref, k_hbm, vt_hbm, o_ref, k_ref, vt_ref, ksem, vsem,
                 s_ref, cm_ref, acc_ref, *, nb, nq):
    b = pl.program_id(0)
    qi = pl.program_id(1)
    chunk = k_ref.shape[1] // KV_CHUNKS
    assert chunk == 4 * TK and (nq - 1) // 4 >= 1
    more_batches = b < nb - 1
    acc_ref[...] = jnp.zeros_like(acc_ref)

    def k_copy(bb, c):
        rows = pl.ds(c * chunk, chunk)
        return pltpu.make_async_copy(k_hbm.at[bb, :, rows, :], k_ref.at[:, rows, :], ksem.at[c])

    def v_copy(bb, c):
        cols = pl.ds(c * chunk, chunk)
        return pltpu.make_async_copy(vt_hbm.at[bb, :, :, cols], vt_ref.at[:, :, cols],
                                     vsem.at[c])

    @pl.when(qi == 0)
    def _():
        @pl.when(b == 0)
        def _():
            k_copy(0, 0).start()
            v_copy(0, 0).start()
            k_copy(0, 0).wait()
        v_copy(b, 0).wait()
        for c in range(1, KV_CHUNKS):
            k_copy(b, c).start()
            v_copy(b, c).start()

    for c in range(1, KV_CHUNKS):
        @pl.when(qi == c * (chunk // TK))
        def _(c=c):
            k_copy(b, c).wait()
            v_copy(b, c).wait()

    def scores(q_ref, t, j):
        start = pl.multiple_of(t * TK, TK)
        return _dot(k_ref[j, pl.ds(start, TK), :], q_ref[0, j])

    def park(j, st):
        s_ref[j] = st
        cm_ref[j] = jnp.max(st, axis=0, keepdims=True)

    def consume(t, j, st, state, cmax):
        m_prev, l_prev = state
        start = pl.multiple_of(t * TK, TK)
        m_new = jnp.maximum(m_prev, cmax)
        alpha = jnp.exp2(m_prev - m_new)
        pt = jnp.exp2(st - m_new)
        l_new = alpha * l_prev + jnp.sum(pt, axis=0, keepdims=True)
        vt = vt_ref[j, :, pl.ds(start, TK)]
        return m_new, l_new, alpha, _dot(vt, pt.astype(jnp.bfloat16))

    @pl.when((qi == 0) & (b == 0))
    def _():
        for j in range(HB):
            park(j, scores(qt_ref, 0, j))

    def body(t, carry):
        out = []
        for j in range(HB):
            m_prev, l_prev = carry[j]
            m_new, l_new, alpha, pv = consume(t, j, s_ref[j], (m_prev, l_prev), cm_ref[j])
            park(j, scores(qt_ref, t + 1, j))
            acc_ref[j] = alpha * acc_ref[j] + pv
            out.append((m_new, l_new))
        return tuple(out)

    init = tuple((jnp.full((1, TQ), NEG_BIG, jnp.float32), jnp.zeros((1, TQ), jnp.float32))
                 for _ in range(HB))
    def tiles(first, count, c):
        for i in range(count):
            c = body(first + i, c)
        return c

    carry = lax.fori_loop(0, qi // 4, lambda u, c: tiles(4 * u, 4, c), init)

    @pl.when((qi == nq - 1) & more_batches)
    def _():
        k_copy(b + 1, 0).start()
        v_copy(b + 1, 0).start()

    carry = lax.cond(qi % 4 >= 2, lambda c: tiles((qi // 4) * 4, 2, c), lambda c: c, carry)
    carry = lax.cond(qi % 2 == 1, lambda c: body(qi - 1, c), lambda c: c, carry)

    @pl.when((qi == nq - 1) & more_batches)
    def _():
        k_copy(b + 1, 0).wait()

    kc = lax.broadcasted_iota(jnp.int32, (LB, LB), 0) // CHUNK
    qc = lax.broadcasted_iota(jnp.int32, (LB, LB), 1) // CHUNK
    diag_ok = kc <= qc
    start = pl.multiple_of(qi * TK, TK)
    for j in range(HB):
        m_prev, l_prev = carry[j]
        p_blocks, m_blocks, l_blocks = [], [], []
        for c in range(TQ // LB):
            lanes = slice(c * LB, (c + 1) * LB)
            sq = jnp.where(diag_ok, s_ref[j, c * LB:(c + 1) * LB, lanes], NEG_BIG)
            if c:
                st = jnp.concatenate([s_ref[j, 0:c * LB, lanes], sq], axis=0)
            else:
                st = sq
            m_new = jnp.maximum(m_prev[:, lanes], jnp.max(st, axis=0, keepdims=True))
            pt = jnp.exp2(st - m_new)
            m_blocks.append(m_new)
            l_blocks.append(jnp.sum(pt, axis=0, keepdims=True))
            pt = pt.astype(jnp.bfloat16)
            if (c + 1) * LB < TK:
                pt = jnp.concatenate(
                    [pt, jnp.zeros((TK - (c + 1) * LB, LB), jnp.bfloat16)], axis=0)
            p_blocks.append(pt)
        m_new = jnp.concatenate(m_blocks, axis=1)
        alpha = jnp.exp2(m_prev - m_new)
        l_new = alpha * l_prev + jnp.concatenate(l_blocks, axis=1)
        pv = _dot(vt_ref[j, :, pl.ds(start, TK)], jnp.concatenate(p_blocks, axis=1))
        park(j, scores(qtn_ref, 0, j))
        ot = (alpha * acc_ref[j] + pv) * (1.0 / l_new)
        o_ref[:, j * DV:(j + 1) * DV] = ot.T.astype(o_ref.dtype)


def _attn_call(qt, k, vt, batch, seq):
    assert TQ == TK and HB == HEADS
    nq = seq // TQ

    def next_tile(b, i):
        last = i == nq - 1
        return (jnp.where(last, jnp.minimum(b + 1, batch - 1), b), 0, 0,
                jnp.where(last, 0, i + 1))

    return pl.pallas_call(
        functools.partial(_attn_kernel, nb=batch, nq=nq),
        grid=(batch, nq),
        in_specs=[
            pl.BlockSpec((1, HB, DQK, TQ), lambda b, i: (b, 0, 0, i)),
            pl.BlockSpec((1, HB, DQK, TQ), next_tile),
            pl.BlockSpec(memory_space=pl.ANY),
            pl.BlockSpec(memory_space=pl.ANY),
        ],
        out_specs=pl.BlockSpec((TQ, HB * DV), lambda b, i: (b * nq + i, 0)),
        out_shape=jax.ShapeDtypeStruct((batch * seq, MLA_W), jnp.bfloat16),
        scratch_shapes=[pltpu.VMEM((HB, seq, DQK), jnp.bfloat16),
                        pltpu.VMEM((HB, DV, seq), jnp.bfloat16),
                        pltpu.SemaphoreType.DMA((KV_CHUNKS,)),
                        pltpu.SemaphoreType.DMA((KV_CHUNKS,)),
                        pltpu.VMEM((HB, TK, TQ), jnp.float32),
                        pltpu.VMEM((HB, 1, TQ), jnp.float32),
                        pltpu.VMEM((HB, DV, TQ), jnp.float32)],
        compiler_params=pltpu.CompilerParams(
            dimension_semantics=("arbitrary", "arbitrary"),
            vmem_limit_bytes=VMEM_LIMIT),
        name="mla_attn",
    )(qt, qt, k, vt)


def _out_kernel(x_ref, o_ref, gate_ref, og_ref, omg_ref, wo_ref, fg_ref, out_ref):
    for r0 in range(0, TM_OUT, SUB_OUT):
        rows = slice(r0, r0 + SUB_OUT)
        h = x_ref[rows, :] + _dot(og_ref[rows, :], wo_ref[MLA_W:, :])
        om = _rms(o_ref[rows, :].astype(jnp.float32), omg_ref[...]) \
            * gate_ref[rows, :].astype(jnp.float32)
        h = h + _dot(om.astype(jnp.bfloat16), wo_ref[0:MLA_W, :])
        out_ref[rows, :] = _rms(h, fg_ref[...])


def _out_call(x2, o_mla, gate, og, omg, wo, fg):
    n = x2.shape[0]
    tm = TM_OUT
    row_spec = pl.BlockSpec((tm, D_MODEL), lambda i: (i, 0))
    return pl.pallas_call(
        _out_kernel,
        grid=(n // tm,),
        in_specs=[row_spec, row_spec, row_spec, row_spec,
                  _const_spec(omg.shape), _const_spec(wo.shape), _const_spec(fg.shape)],
        out_specs=row_spec,
        out_shape=jax.ShapeDtypeStruct((n, D_MODEL), jnp.float32),
        compiler_params=pltpu.CompilerParams(
            dimension_semantics=("parallel",), vmem_limit_bytes=VMEM_LIMIT),
        name="out_proj",
    )(x2, o_mla, gate, og, omg, wo, fg)


def _rope_tables(seq):
    inv_freq = ROPE_THETA ** (-jnp.arange(0, ROPE, 2, dtype=jnp.float32) / ROPE)
    hi = jnp.arange(0, seq, POS_SPLIT, dtype=jnp.float32)[:, None] * inv_freq[None, :]
    lo = jnp.arange(POS_SPLIT, dtype=jnp.float32)[:, None] * inv_freq[None, :]
    ch, sh = jnp.cos(hi)[:, None, :], jnp.sin(hi)[:, None, :]
    cl, sl = jnp.cos(lo)[None, :, :], jnp.sin(lo)[None, :, :]
    cos = (ch * cl - sh * sl).reshape(seq, ROPE // 2)
    sin = (sh * cl + ch * sl).reshape(seq, ROPE // 2)
    ones = jnp.ones((NOPE, seq), jnp.float32)
    zeros_n = jnp.zeros((NOPE, seq), jnp.float32)
    pad = jnp.zeros((DQK - NOPE - ROPE, seq), jnp.float32)
    cos_t = jnp.concatenate([ones, cos.T, cos.T, pad], axis=0)
    sin_t = jnp.concatenate([zeros_n, sin.T, sin.T, pad], axis=0)
    return cos_t, sin_t


def _rot_half_cols(w):
    half = ROPE // 2
    return jnp.concatenate([-w[..., half:], w[..., :half]], axis=-1)


def _pack_big_kernel(wt_ref, wbig_ref):
    wbig_ref[...] = wt_ref[...].T.astype(jnp.bfloat16)


def _pack_lat_kernel(wt_ref, wlat_ref):
    w = wt_ref[...]
    kr = w[L_KR:L_KR + ROPE]
    half = ROPE // 2
    blk = jnp.concatenate([w[:L_KR], jnp.zeros((NOPE, w.shape[1]), w.dtype), kr,
                           -kr[half:], kr[:half]], axis=0)
    wlat_ref[...] = blk.T.astype(jnp.bfloat16)


def _pack_w_in(w_in):
    d, d_in = w_in.shape
    wt = w_in.T
    n_lat = L_KR + ROPE
    wbig = pl.pallas_call(
        _pack_big_kernel,
        grid=(D_BIG // PACK_COLS,),
        in_specs=[pl.BlockSpec((pl.Element(PACK_COLS), pl.Element(d)),
                               lambda j: (pl.multiple_of(n_lat + j * PACK_COLS, ROPE), 0))],
        out_specs=pl.BlockSpec((d, PACK_COLS), lambda j: (0, j)),
        out_shape=jax.ShapeDtypeStruct((d, D_BIG), jnp.bfloat16),
        compiler_params=pltpu.CompilerParams(dimension_semantics=("parallel",)),
        name="pack_w_big",
    )(wt)
    wlat = pl.pallas_call(
        _pack_lat_kernel,
        grid=(1,),
        in_specs=[pl.BlockSpec((n_lat, d), lambda j: (0, 0))],
        out_specs=pl.BlockSpec((d, D_LAT), lambda j: (0, 0)),
        out_shape=jax.ShapeDtypeStruct((d, D_LAT), jnp.bfloat16),
        name="pack_w_lat",
    )(wt)
    return wlat, wbig


def _pack_weights(w_in, w_uq, w_ukv):
    bf = jnp.bfloat16
    wlat, wbig = _pack_w_in(w_in)
    wq = w_uq.reshape(Q_RANK, HEADS, NOPE + ROPE)
    wuq = jnp.concatenate([wq, _rot_half_cols(wq[..., NOPE:])],
                          axis=-1).reshape(Q_RANK, HEADS * DQK).T.astype(bf)
    wkv = w_ukv.reshape(KV_RANK, HEADS, NOPE + DV)
    wkn = jnp.concatenate([wkv[..., :NOPE], jnp.zeros((KV_RANK, HEADS, DQK - NOPE), w_ukv.dtype)],
                          axis=-1).reshape(KV_RANK, HEADS * DQK).astype(bf)
    wv = wkv[..., NOPE:].reshape(KV_RANK, HEADS * DV).T.astype(bf)
    return wlat, wbig, wuq, wkn, wv


def kernel(x, norm_in_g, w_in, q_norm_g, w_uq, kv_norm_g, w_ukv, gmlp_ln_g, gmlp_ln_b,
           w_spatial, b_spatial, out_norm_mla_g, out_norm_gmlp_g, w_out, final_norm_g):
    batch, seq, d = x.shape
    n = batch * seq
    x2 = x.reshape(n, d)
    row = lambda a: a.reshape(1, -1).astype(jnp.float32)

    wlat, wbig, wuq, wkn, wv = _pack_weights(w_in, w_uq, w_ukv)
    ws = w_spatial.astype(jnp.float32)
    bs = jnp.broadcast_to(b_spatial[:, :, None], (GROUPS, SBLOCK, GDIM)).astype(jnp.float32)
    consts = (row(norm_in_g), wlat, wbig, row(q_norm_g), wuq, row(kv_norm_g), wkn, wv,
              row(gmlp_ln_g), row(gmlp_ln_b), ws, bs, row(out_norm_gmlp_g))

    qt, k, vt, gate, og = _proj_call(x2, _rope_tables(seq), consts, batch, seq)
    o_mla = _attn_call(qt, k, vt, batch, seq)
    out = _out_call(x2, o_mla, gate, og, row(out_norm_mla_g), w_out.astype(jnp.bfloat16),
                    row(final_norm_g))
    return out.reshape(batch, seq, d)
```

```python
import functools
import math

import jax
import jax.numpy as jnp
from jax import lax
from jax.experimental import pallas as pl
from jax.experimental.pallas import tpu as pltpu

D_MODEL = 1024
CHUNK = 64
EPS = 1e-6

HEADS = 8
NOPE = 64
ROPE = 32
DV = 128
DQK = 128
Q_RANK = 384
KV_RANK = 256
MLA_W = HEADS * DV
ROPE_THETA = 10000.0
QK_SCALE = math.log2(math.e) / math.sqrt(NOPE + ROPE)

GROUPS = 8
GDIM = 128
GMLP_W = GROUPS * GDIM
SBLOCK = 128

L_QLAT = 0
L_KVLAT = L_QLAT + Q_RANK
L_KR = L_KVLAT + KV_RANK
D_LAT = L_KR + DQK
B_GMLA = 0
B_U = B_GMLA + MLA_W
B_V = B_U + GMLP_W
B_GG = B_V + GMLP_W
D_BIG = B_GG + GMLP_W

VMEM_LIMIT = 56 * 1024 * 1024

TM_PROJ = 512
SUB_PROJ = 256
TQ = 512
TK = 512
HB = 8
KV_CHUNKS = 4
LB = 128
TM_OUT = 1024
PACK_COLS = 512
POS_SPLIT = 128
SUB_OUT = 256

NEG_BIG = -1e30


def _rms(x, g, eps=EPS):
    return x * lax.rsqrt(jnp.mean(x * x, axis=-1, keepdims=True) + eps) * g


def _dot(a, b):
    return jnp.dot(a, b, preferred_element_type=jnp.float32)


def _silu(x):
    return x * (1.0 / (1.0 + jnp.exp2(x * (-math.log2(math.e)))))


def _gelu_x2(x):
    return x * (1.0 + lax.erf(x * (1.0 / math.sqrt(2.0))))


def _rope(x, cos, sin):
    return x * cos + pltpu.roll(x, DQK - ROPE, 1) * sin


def _proj_kernel(x_ref, cost_ref, sint_ref, ng_ref, wlat_ref, wbig_ref,
                 qg_ref, wuqt_ref, kvg_ref, wkn_ref, wvt_ref, lng_ref, lnb_ref, ws_ref, bs_ref,
                 ogg_ref, qt_ref, k_ref, vt_ref, gate_ref, og_ref):
    tc = lax.broadcasted_iota(jnp.int32, (SBLOCK, SBLOCK), 0) // CHUNK
    sc = lax.broadcasted_iota(jnp.int32, (SBLOCK, SBLOCK), 1) // CHUNK
    ws = [jnp.where(sc <= tc, ws_ref[g], 0.0).astype(jnp.bfloat16) for g in range(GROUPS)]
    nblk = SUB_PROJ // SBLOCK

    for r0 in range(0, TM_PROJ, SUB_PROJ):
        rows = slice(r0, r0 + SUB_PROJ)
        y = _rms(x_ref[rows, :], ng_ref[...]).astype(jnp.bfloat16)

        v = _gelu_x2(_dot(y, wbig_ref[:, B_V:B_GG]))
        u = _gelu_x2(_dot(y, wbig_ref[:, B_U:B_V]))
        mu = jnp.mean(v, axis=-1, keepdims=True)
        vc = v - mu
        var = jnp.mean(vc * vc, axis=-1, keepdims=True)
        vln = (vc * lax.rsqrt(var + 4.0 * EPS) * lng_ref[...] + lnb_ref[...]).astype(jnp.bfloat16)

        lat = _dot(y, wlat_ref[...])
        gate_ref[rows, :] = _silu(_dot(y, wbig_ref[:, B_GMLA:B_U])).astype(jnp.bfloat16)
        ggs = _silu(_dot(y, wbig_ref[:, B_GG:D_BIG]))

        mixed_g = []
        for g in range(GROUPS):
            cols = slice(g * GDIM, (g + 1) * GDIM)
            vb = jnp.concatenate([vln[c * SBLOCK:(c + 1) * SBLOCK, cols] for c in range(nblk)],
                                 axis=1)
            mixed_g.append(_dot(ws[g], vb))
        mixed = jnp.concatenate(
            [jnp.concatenate([mixed_g[g][:, c * GDIM:(c + 1) * GDIM] + bs_ref[g]
                              for g in range(GROUPS)], axis=1) for c in range(nblk)], axis=0)
        og_ref[rows, :] = (_rms(u * mixed, ogg_ref[...], 4.0 * EPS) * ggs).astype(jnp.bfloat16)

        qn = _rms(lat[:, L_QLAT:L_KVLAT], qg_ref[...])
        kvn = _rms(lat[:, L_KVLAT:L_KR], kvg_ref[...])
        cos_t = cost_ref[:, rows]
        sin_t = sint_ref[:, rows]
        kr = _rope(lat[:, L_KR:D_LAT], cos_t.T, sin_t.T)
        k_all = _dot(kvn.astype(jnp.bfloat16), wkn_ref[...])
        qt_all = _dot(wuqt_ref[...], qn.T.astype(jnp.bfloat16))
        vt_all = _dot(wvt_ref[...], kvn.T.astype(jnp.bfloat16))
        cosq = cos_t * QK_SCALE
        sinq = sin_t * QK_SCALE
        for h in range(HEADS):
            sl = slice(h * DQK, (h + 1) * DQK)
            qb = qt_all[sl, :]
            rot = jnp.concatenate([qb[:NOPE], qb[NOPE + ROPE:], qb[NOPE + ROPE:]], axis=0)
            qt_ref[0, h, :, rows] = (qb * cosq + rot * sinq).astype(jnp.bfloat16)
            k_ref[0, h, rows, :] = (k_all[:, sl] + kr).astype(jnp.bfloat16)
            vt_ref[0, h, :, rows] = vt_all[sl, :].astype(jnp.bfloat16)


def _const_spec(shape):
    nd = len(shape)
    return pl.BlockSpec(shape, lambda i, _nd=nd: (0,) * _nd, pipeline_mode=pl.Buffered(1))


def _proj_call(x2, tables, consts, batch, seq):
    n = x2.shape[0]
    tm = TM_PROJ
    tiles_per_seq = seq // tm
    row_spec = pl.BlockSpec((tm, D_MODEL), lambda i: (i, 0))
    tab_spec = pl.BlockSpec((DQK, tm), lambda i: (0, i % tiles_per_seq))
    head_spec = pl.BlockSpec((1, HEADS, tm, DQK),
                             lambda i: (i // tiles_per_seq, 0, i % tiles_per_seq, 0))
    head_shape = jax.ShapeDtypeStruct((batch, HEADS, seq, DQK), jnp.bfloat16)
    headt_spec = pl.BlockSpec((1, HEADS, DQK, tm),
                              lambda i: (i // tiles_per_seq, 0, 0, i % tiles_per_seq))
    headt_shape = jax.ShapeDtypeStruct((batch, HEADS, DQK, seq), jnp.bfloat16)
    row_shape = jax.ShapeDtypeStruct((n, D_MODEL), jnp.bfloat16)
    return pl.pallas_call(
        _proj_kernel,
        grid=(n // tm,),
        in_specs=[row_spec] + [tab_spec] * len(tables) + [_const_spec(c.shape) for c in consts],
        out_specs=[headt_spec, head_spec, headt_spec, row_spec, row_spec],
        out_shape=[headt_shape, head_shape, headt_shape, row_shape, row_shape],
        compiler_params=pltpu.CompilerParams(
            dimension_semantics=("parallel",), vmem_limit_bytes=VMEM_LIMIT),
        name="proj_gmlp",
    )(x2, *tables, *consts)


def _attn_kernel(qt_ref, qtn_ref, k_hbm, vt_hbm, o_ref, k_ref, vt_ref, ksem, vsem,
                 s_ref, cm_ref, acc_ref, *, nb, nq):
    b = pl.program_id(0)
    qi = pl.program_id(1)
    chunk = k_ref.shape[1] // KV_CHUNKS
    assert chunk == 4 * TK and (nq - 1) // 4 >= 2
    more_batches = b < nb - 1
    acc_ref[...] = jnp.zeros_like(acc_ref)

    def k_copy(bb, c):
        rows = pl.ds(c * chunk, chunk)
        return pltpu.make_async_copy(k_hbm.at[bb, :, rows, :], k_ref.at[:, rows, :], ksem.at[c])

    def v_copy(bb, c):
        cols = pl.ds(c * chunk, chunk)
        return pltpu.make_async_copy(vt_hbm.at[bb, :, :, cols], vt_ref.at[:, :, cols],
                                     vsem.at[c])

    @pl.when(qi == 0)
    def _():
        @pl.when(b == 0)
        def _():
            k_copy(0, 0).start()
            v_copy(0, 0).start()
            k_copy(0, 0).wait()
        v_copy(b, 0).wait()
        for c in range(1, KV_CHUNKS):
            k_copy(b, c).start()
            v_copy(b, c).start()

    for c in range(1, KV_CHUNKS):
        @pl.when(qi == c * (chunk // TK))
        def _(c=c):
            k_copy(b, c).wait()
            v_copy(b, c).wait()

    def scores(q_ref, t, j):
        start = pl.multiple_of(t * TK, TK)
        return _dot(k_ref[j, pl.ds(start, TK), :], q_ref[0, j])

    def park(j, st):
        s_ref[j] = st
        cm_ref[j] = jnp.max(st, axis=0, keepdims=True)

    def consume(t, j, st, state, cmax):
        m_prev, l_prev = state
        start = pl.multiple_of(t * TK, TK)
        m_new = jnp.maximum(m_prev, cmax)
        alpha = jnp.exp2(m_prev - m_new)
        pt = jnp.exp2(st - m_new)
        l_new = alpha * l_prev + jnp.sum(pt, axis=0, keepdims=True)
        vt = vt_ref[j, :, pl.ds(start, TK)]
        return m_new, l_new, alpha, _dot(vt, pt.astype(jnp.bfloat16))

    @pl.when((qi == 0) & (b == 0))
    def _():
        for j in range(HB):
            park(j, scores(qt_ref, 0, j))

    def body(t, carry):
        out = []
        for j in range(HB):
            m_prev, l_prev = carry[j]
            m_new, l_new, alpha, pv = consume(t, j, s_ref[j], (m_prev, l_prev), cm_ref[j])
            park(j, scores(qt_ref, t + 1, j))
            acc_ref[j] = alpha * acc_ref[j] + pv
            out.append((m_new, l_new))
        return tuple(out)

    def diagonal(carry):
        kc = lax.broadcasted_iota(jnp.int32, (LB, LB), 0) // CHUNK
        qc = lax.broadcasted_iota(jnp.int32, (LB, LB), 1) // CHUNK
        diag_ok = kc <= qc
        start = pl.multiple_of(qi * TK, TK)
        for j in range(HB):
            m_prev, l_prev = carry[j]
            p_blocks, m_blocks, l_blocks = [], [], []
            for c in range(TQ // LB):
                lanes = slice(c * LB, (c + 1) * LB)
                sq = jnp.where(diag_ok, s_ref[j, c * LB:(c + 1) * LB, lanes], NEG_BIG)
                if c:
                    st = jnp.concatenate([s_ref[j, 0:c * LB, lanes], sq], axis=0)
                else:
                    st = sq
                m_new = jnp.maximum(m_prev[:, lanes], jnp.max(st, axis=0, keepdims=True))
                pt = jnp.exp2(st - m_new)
                m_blocks.append(m_new)
                l_blocks.append(jnp.sum(pt, axis=0, keepdims=True))
                pt = pt.astype(jnp.bfloat16)
                if (c + 1) * LB < TK:
                    pt = jnp.concatenate(
                        [pt, jnp.zeros((TK - (c + 1) * LB, LB), jnp.bfloat16)], axis=0)
                p_blocks.append(pt)
            m_new = jnp.concatenate(m_blocks, axis=1)
            alpha = jnp.exp2(m_prev - m_new)
            l_new = alpha * l_prev + jnp.concatenate(l_blocks, axis=1)
            pv = _dot(vt_ref[j, :, pl.ds(start, TK)], jnp.concatenate(p_blocks, axis=1))
            park(j, scores(qtn_ref, 0, j))
            ot = (alpha * acc_ref[j] + pv) * (1.0 / l_new)
            o_ref[:, j * DV:(j + 1) * DV] = ot.T.astype(o_ref.dtype)

    init = tuple((jnp.full((1, TQ), NEG_BIG, jnp.float32), jnp.zeros((1, TQ), jnp.float32))
                 for _ in range(HB))
    def tiles(first, count, c):
        for i in range(count):
            c = body(first + i, c)
        return c

    def loop_step(u, c):
        @pl.when((qi == nq - 1) & more_batches & (u == 1))
        def _():
            k_copy(b + 1, 0).start()
            v_copy(b + 1, 0).start()

        return tiles(4 * u, 4, c)

    carry = lax.fori_loop(0, qi // 4, loop_step, init)

    @pl.when((qi == nq - 1) & more_batches)
    def _():
        k_copy(b + 1, 0).wait()

    rem = qi % 4

    def finish(count, c):
        diagonal(tiles((qi // 4) * 4, count, c))
        return 0

    lax.cond(rem >= 2,
             lambda c: lax.cond(rem == 3, functools.partial(finish, 3),
                                functools.partial(finish, 2), c),
             lambda c: lax.cond(rem == 1, functools.partial(finish, 1),
                                functools.partial(finish, 0), c),
             carry)


def _attn_call(qt, k, vt, batch, seq):
    assert TQ == TK and HB == HEADS
    nq = seq // TQ

    def next_tile(b, i):
        last = i == nq - 1
        return (jnp.where(last, jnp.minimum(b + 1, batch - 1), b), 0, 0,
                jnp.where(last, 0, i + 1))

    return pl.pallas_call(
        functools.partial(_attn_kernel, nb=batch, nq=nq),
        grid=(batch, nq),
        in_specs=[
            pl.BlockSpec((1, HB, DQK, TQ), lambda b, i: (b, 0, 0, i)),
            pl.BlockSpec((1, HB, DQK, TQ), next_tile),
            pl.BlockSpec(memory_space=pl.ANY),
            pl.BlockSpec(memory_space=pl.ANY),
        ],
        out_specs=pl.BlockSpec((TQ, HB * DV), lambda b, i: (b * nq + i, 0)),
        out_shape=jax.ShapeDtypeStruct((batch * seq, MLA_W), jnp.bfloat16),
        scratch_shapes=[pltpu.VMEM((HB, seq, DQK), jnp.bfloat16),
                        pltpu.VMEM((HB, DV, seq), jnp.bfloat16),
                        pltpu.SemaphoreType.DMA((KV_CHUNKS,)),
                        pltpu.SemaphoreType.DMA((KV_CHUNKS,)),
                        pltpu.VMEM((HB, TK, TQ), jnp.float32),
                        pltpu.VMEM((HB, 1, TQ), jnp.float32),
                        pltpu.VMEM((HB, DV, TQ), jnp.float32)],
        compiler_params=pltpu.CompilerParams(
            dimension_semantics=("arbitrary", "arbitrary"),
            vmem_limit_bytes=VMEM_LIMIT),
        name="mla_attn",
    )(qt, qt, k, vt)


def _out_kernel(x_ref, o_ref, gate_ref, og_ref, omg_ref, wo_ref, fg_ref, out_ref):
    for r0 in range(0, TM_OUT, SUB_OUT):
        rows = slice(r0, r0 + SUB_OUT)
        h = x_ref[rows, :] + _dot(og_ref[rows, :], wo_ref[MLA_W:, :])
        om = _rms(o_ref[rows, :].astype(jnp.float32), omg_ref[...]) \
            * gate_ref[rows, :].astype(jnp.float32)
        h = h + _dot(om.astype(jnp.bfloat16), wo_ref[0:MLA_W, :])
        out_ref[rows, :] = _rms(h, fg_ref[...])


def _out_call(x2, o_mla, gate, og, omg, wo, fg):
    n = x2.shape[0]
    tm = TM_OUT
    row_spec = pl.BlockSpec((tm, D_MODEL), lambda i: (i, 0))
    return pl.pallas_call(
        _out_kernel,
        grid=(n // tm,),
        in_specs=[row_spec, row_spec, row_spec, row_spec,
                  _const_spec(omg.shape), _const_spec(wo.shape), _const_spec(fg.shape)],
        out_specs=row_spec,
        out_shape=jax.ShapeDtypeStruct((n, D_MODEL), jnp.float32),
        compiler_params=pltpu.CompilerParams(
            dimension_semantics=("parallel",), vmem_limit_bytes=VMEM_LIMIT),
        name="out_proj",
    )(x2, o_mla, gate, og, omg, wo, fg)


def _rope_tables(seq):
    inv_freq = ROPE_THETA ** (-jnp.arange(0, ROPE, 2, dtype=jnp.float32) / ROPE)
    hi = jnp.arange(0, seq, POS_SPLIT, dtype=jnp.float32)[:, None] * inv_freq[None, :]
    lo = jnp.arange(POS_SPLIT, dtype=jnp.float32)[:, None] * inv_freq[None, :]
    ch, sh = jnp.cos(hi)[:, None, :], jnp.sin(hi)[:, None, :]
    cl, sl = jnp.cos(lo)[None, :, :], jnp.sin(lo)[None, :, :]
    cos = (ch * cl - sh * sl).reshape(seq, ROPE // 2)
    sin = (sh * cl + ch * sl).reshape(seq, ROPE // 2)
    ones = jnp.ones((NOPE, seq), jnp.float32)
    zeros_n = jnp.zeros((NOPE, seq), jnp.float32)
    pad = jnp.zeros((DQK - NOPE - ROPE, seq), jnp.float32)
    cos_t = jnp.concatenate([ones, cos.T, cos.T, pad], axis=0)
    sin_t = jnp.concatenate([zeros_n, sin.T, sin.T, pad], axis=0)
    return cos_t, sin_t


def _rot_half_cols(w):
    half = ROPE // 2
    return jnp.concatenate([-w[..., half:], w[..., :half]], axis=-1)


def _pack_big_kernel(wt_ref, wbig_ref):
    wbig_ref[...] = wt_ref[...].T.astype(jnp.bfloat16)


def _pack_lat_kernel(wt_ref, wlat_ref):
    w = wt_ref[...]
    kr = w[L_KR:L_KR + ROPE]
    half = ROPE // 2
    blk = jnp.concatenate([w[:L_KR], jnp.zeros((NOPE, w.shape[1]), w.dtype), kr,
                           -kr[half:], kr[:half]], axis=0)
    wlat_ref[...] = blk.T.astype(jnp.bfloat16)


def _pack_w_in(w_in):
    d, d_in = w_in.shape
    wt = w_in.T
    n_lat = L_KR + ROPE
    wbig = pl.pallas_call(
        _pack_big_kernel,
        grid=(D_BIG // PACK_COLS,),
        in_specs=[pl.BlockSpec((pl.Element(PACK_COLS), pl.Element(d)),
                               lambda j: (pl.multiple_of(n_lat + j * PACK_COLS, ROPE), 0))],
        out_specs=pl.BlockSpec((d, PACK_COLS), lambda j: (0, j)),
        out_shape=jax.ShapeDtypeStruct((d, D_BIG), jnp.bfloat16),
        compiler_params=pltpu.CompilerParams(dimension_semantics=("parallel",)),
        name="pack_w_big",
    )(wt)
    wlat = pl.pallas_call(
        _pack_lat_kernel,
        grid=(1,),
        in_specs=[pl.BlockSpec((n_lat, d), lambda j: (0, 0))],
        out_specs=pl.BlockSpec((d, D_LAT), lambda j: (0, 0)),
        out_shape=jax.ShapeDtypeStruct((d, D_LAT), jnp.bfloat16),
        name="pack_w_lat",
    )(wt)
    return wlat, wbig


def _pack_weights(w_in, w_uq, w_ukv):
    bf = jnp.bfloat16
    wlat, wbig = _pack_w_in(w_in)
    wq = w_uq.reshape(Q_RANK, HEADS, NOPE + ROPE)
    wuq = jnp.concatenate([wq, _rot_half_cols(wq[..., NOPE:])],
                          axis=-1).reshape(Q_RANK, HEADS * DQK).T.astype(bf)
    wkv = w_ukv.reshape(KV_RANK, HEADS, NOPE + DV)
    wkn = jnp.concatenate([wkv[..., :NOPE], jnp.zeros((KV_RANK, HEADS, DQK - NOPE), w_ukv.dtype)],
                          axis=-1).reshape(KV_RANK, HEADS * DQK).astype(bf)
    wv = wkv[..., NOPE:].reshape(KV_RANK, HEADS * DV).T.astype(bf)
    return wlat, wbig, wuq, wkn, wv


def kernel(x, norm_in_g, w_in, q_norm_g, w_uq, kv_norm_g, w_ukv, gmlp_ln_g, gmlp_ln_b,
           w_spatial, b_spatial, out_norm_mla_g, out_norm_gmlp_g, w_out, final_norm_g):
    batch, seq, d = x.shape
    n = batch * seq
    x2 = x.reshape(n, d)
    row = lambda a: a.reshape(1, -1).astype(jnp.float32)

    wlat, wbig, wuq, wkn, wv = _pack_weights(w_in, w_uq, w_ukv)
    ws = w_spatial.astype(jnp.float32)
    bs = jnp.broadcast_to(b_spatial[:, :, None], (GROUPS, SBLOCK, GDIM)).astype(jnp.float32)
    consts = (row(norm_in_g), wlat, wbig, row(q_norm_g), wuq, row(kv_norm_g), wkn, wv,
              row(gmlp_ln_g), row(gmlp_ln_b), ws, bs, row(out_norm_gmlp_g))

    qt, k, vt, gate, og = _proj_call(x2, _rope_tables(seq), consts, batch, seq)
    o_mla = _attn_call(qt, k, vt, batch, seq)
    out = _out_call(x2, o_mla, gate, og, row(out_norm_mla_g), w_out.astype(jnp.bfloat16),
                    row(final_norm_g))
    return out.reshape(batch, seq, d)
```

```python
import functools
import math

import jax
import jax.numpy as jnp
from jax import lax
from jax.experimental import pallas as pl
from jax.experimental.pallas import tpu as pltpu

D_MODEL = 1024
CHUNK = 64
EPS = 1e-6

HEADS = 8
NOPE = 64
ROPE = 32
DV = 128
DQK = 128
Q_RANK = 384
KV_RANK = 256
MLA_W = HEADS * DV
ROPE_THETA = 10000.0
QK_SCALE = math.log2(math.e) / math.sqrt(NOPE + ROPE)

GROUPS = 8
GDIM = 128
GMLP_W = GROUPS * GDIM
SBLOCK = 128

L_QLAT = 0
L_KVLAT = L_QLAT + Q_RANK
L_KR = L_KVLAT + KV_RANK
D_LAT = L_KR + DQK
B_GMLA = 0
B_U = B_GMLA + MLA_W
B_V = B_U + GMLP_W
B_GG = B_V + GMLP_W
D_BIG = B_GG + GMLP_W

VMEM_LIMIT = 56 * 1024 * 1024

TM_PROJ = 512
SUB_PROJ = 256
TQ = 512
TK = 512
HB = 8
KV_CHUNKS = 4
LOOP_TILES = 4
LB = 128
TM_OUT = 1024
PACK_COLS = 1024
POS_SPLIT = 128
SUB_OUT = 256

NEG_BIG = -1e30


def _rms(x, g, eps=EPS):
    return x * lax.rsqrt(jnp.mean(x * x, axis=-1, keepdims=True) + eps) * g


def _dot(a, b):
    return jnp.dot(a, b, preferred_element_type=jnp.float32)


def _silu(x):
    return x * (1.0 / (1.0 + jnp.exp2(x * (-math.log2(math.e)))))


def _gelu_x2(x):
    return x * (1.0 + lax.erf(x * (1.0 / math.sqrt(2.0))))


def _rope(x, cos, sin):
    return x * cos + pltpu.roll(x, DQK - ROPE, 1) * sin


def _proj_kernel(x_ref, cost_ref, sint_ref, ng_ref, wlat_ref, wbig_ref,
                 qg_ref, wuqt_ref, kvg_ref, wkn_ref, wvt_ref, lng_ref, lnb_ref, ws_ref, bs_ref,
                 ogg_ref, qt_ref, k_ref, vt_ref, gate_ref, og_ref):
    tc = lax.broadcasted_iota(jnp.int32, (SBLOCK, SBLOCK), 0) // CHUNK
    sc = lax.broadcasted_iota(jnp.int32, (SBLOCK, SBLOCK), 1) // CHUNK
    ws = [jnp.where(sc <= tc, ws_ref[g], 0.0).astype(jnp.bfloat16) for g in range(GROUPS)]
    nblk = SUB_PROJ // SBLOCK

    for r0 in range(0, TM_PROJ, SUB_PROJ):
        rows = slice(r0, r0 + SUB_PROJ)
        y = _rms(x_ref[rows, :], ng_ref[...]).astype(jnp.bfloat16)

        v = _gelu_x2(_dot(y, wbig_ref[:, B_V:B_GG]))
        u = _gelu_x2(_dot(y, wbig_ref[:, B_U:B_V]))
        mu = jnp.mean(v, axis=-1, keepdims=True)
        vc = v - mu
        var = jnp.mean(vc * vc, axis=-1, keepdims=True)
        vln = (vc * lax.rsqrt(var + 4.0 * EPS) * lng_ref[...] + lnb_ref[...]).astype(jnp.bfloat16)

        lat = _dot(y, wlat_ref[...])
        gate_ref[rows, :] = _silu(_dot(y, wbig_ref[:, B_GMLA:B_U])).astype(jnp.bfloat16)
        ggs = _silu(_dot(y, wbig_ref[:, B_GG:D_BIG]))

        mixed_g = []
        for g in range(GROUPS):
            cols = slice(g * GDIM, (g + 1) * GDIM)
            vb = jnp.concatenate([vln[c * SBLOCK:(c + 1) * SBLOCK, cols] for c in range(nblk)],
                                 axis=1)
            mixed_g.append(_dot(ws[g], vb))
        mixed = jnp.concatenate(
            [jnp.concatenate([mixed_g[g][:, c * GDIM:(c + 1) * GDIM] + bs_ref[g]
                              for g in range(GROUPS)], axis=1) for c in range(nblk)], axis=0)
        og_ref[rows, :] = (_rms(u * mixed, ogg_ref[...], 4.0 * EPS) * ggs).astype(jnp.bfloat16)

        qn = _rms(lat[:, L_QLAT:L_KVLAT], qg_ref[...])
        kvn = _rms(lat[:, L_KVLAT:L_KR], kvg_ref[...])
        cos_t = cost_ref[:, rows]
        sin_t = sint_ref[:, rows]
        kr = _rope(lat[:, L_KR:D_LAT], cos_t.T, sin_t.T)
        k_all = _dot(kvn.astype(jnp.bfloat16), wkn_ref[...])
        qt_all = _dot(wuqt_ref[...], qn.T.astype(jnp.bfloat16))
        vt_all = _dot(wvt_ref[...], kvn.T.astype(jnp.bfloat16))
        cosq = cos_t * QK_SCALE
        sinq = sin_t * QK_SCALE
        for h in range(HEADS):
            sl = slice(h * DQK, (h + 1) * DQK)
            qb = qt_all[sl, :]
            rot = jnp.concatenate([qb[:NOPE], qb[NOPE + ROPE:], qb[NOPE + ROPE:]], axis=0)
            qt_ref[0, h, :, rows] = (qb * cosq + rot * sinq).astype(jnp.bfloat16)
            k_ref[0, h, rows, :] = (k_all[:, sl] + kr).astype(jnp.bfloat16)
            vt_ref[0, h, :, rows] = vt_all[sl, :].astype(jnp.bfloat16)


def _const_spec(shape):
    nd = len(shape)
    return pl.BlockSpec(shape, lambda i, _nd=nd: (0,) * _nd, pipeline_mode=pl.Buffered(1))


def _proj_call(x2, tables, consts, batch, seq):
    n = x2.shape[0]
    tm = TM_PROJ
    tiles_per_seq = seq // tm
    row_spec = pl.BlockSpec((tm, D_MODEL), lambda i: (i, 0))
    tab_spec = pl.BlockSpec((DQK, tm), lambda i: (0, i % tiles_per_seq))
    head_spec = pl.BlockSpec((1, HEADS, tm, DQK),
                             lambda i: (i // tiles_per_seq, 0, i % tiles_per_seq, 0))
    head_shape = jax.ShapeDtypeStruct((batch, HEADS, seq, DQK), jnp.bfloat16)
    headt_spec = pl.BlockSpec((1, HEADS, DQK, tm),
                              lambda i: (i // tiles_per_seq, 0, 0, i % tiles_per_seq))
    headt_shape = jax.ShapeDtypeStruct((batch, HEADS, DQK, seq), jnp.bfloat16)
    row_shape = jax.ShapeDtypeStruct((n, D_MODEL), jnp.bfloat16)
    return pl.pallas_call(
        _proj_kernel,
        grid=(n // tm,),
        in_specs=[row_spec] + [tab_spec] * len(tables) + [_const_spec(c.shape) for c in consts],
        out_specs=[headt_spec, head_spec, headt_spec, row_spec, row_spec],
        out_shape=[headt_shape, head_shape, headt_shape, row_shape, row_shape],
        compiler_params=pltpu.CompilerParams(
            dimension_semantics=("parallel",), vmem_limit_bytes=VMEM_LIMIT),
        name="proj_gmlp",
    )(x2, *tables, *consts)


def _attn_kernel(qt_ref, qtn_ref, k_hbm, vt_hbm, o_ref, k_ref, vt_ref, ksem, vsem,
                 s_ref, cm_ref, acc_ref, *, nb, nq):
    b = pl.program_id(0)
    qi = pl.program_id(1)
    chunk = k_ref.shape[1] // KV_CHUNKS
    assert chunk == LOOP_TILES * TK and (nq - 1) // LOOP_TILES >= 2
    more_batches = b < nb - 1
    acc_ref[...] = jnp.zeros_like(acc_ref)

    def k_copy(bb, c):
        rows = pl.ds(c * chunk, chunk)
        return pltpu.make_async_copy(k_hbm.at[bb, :, rows, :], k_ref.at[:, rows, :], ksem.at[c])

    def v_copy(bb, c):
        cols = pl.ds(c * chunk, chunk)
        return pltpu.make_async_copy(vt_hbm.at[bb, :, :, cols], vt_ref.at[:, :, cols],
                                     vsem.at[c])

    @pl.when(qi == 0)
    def _():
        @pl.when(b == 0)
        def _():
            k_copy(0, 0).start()
            v_copy(0, 0).start()
            k_copy(0, 0).wait()
        v_copy(b, 0).wait()
        for c in range(1, KV_CHUNKS):
            k_copy(b, c).start()
            v_copy(b, c).start()

    for c in range(1, KV_CHUNKS):
        @pl.when(qi == c * (chunk // TK))
        def _(c=c):
            k_copy(b, c).wait()
            v_copy(b, c).wait()

    def scores(q_ref, t, j):
        start = pl.multiple_of(t * TK, TK)
        return _dot(k_ref[j, pl.ds(start, TK), :], q_ref[0, j])

    def park(j, st):
        s_ref[j] = st
        cm_ref[j] = jnp.max(st, axis=0, keepdims=True)

    def consume(t, j, st, state, cmax):
        m_prev, l_prev = state
        start = pl.multiple_of(t * TK, TK)
        m_new = jnp.maximum(m_prev, cmax)
        alpha = jnp.exp2(m_prev - m_new)
        pt = jnp.exp2(st - m_new)
        l_new = alpha * l_prev + jnp.sum(pt, axis=0, keepdims=True)
        vt = vt_ref[j, :, pl.ds(start, TK)]
        return m_new, l_new, alpha, _dot(vt, pt.astype(jnp.bfloat16))

    @pl.when((qi == 0) & (b == 0))
    def _():
        for j in range(HB):
            park(j, scores(qt_ref, 0, j))

    def body(t, carry):
        out = []
        for j in range(HB):
            m_prev, l_prev = carry[j]
            m_new, l_new, alpha, pv = consume(t, j, s_ref[j], (m_prev, l_prev), cm_ref[j])
            park(j, scores(qt_ref, t + 1, j))
            acc_ref[j] = alpha * acc_ref[j] + pv
            out.append((m_new, l_new))
        return tuple(out)

    def diagonal(carry):
        kc = lax.broadcasted_iota(jnp.int32, (LB, LB), 0) // CHUNK
        qc = lax.broadcasted_iota(jnp.int32, (LB, LB), 1) // CHUNK
        diag_ok = kc <= qc
        start = pl.multiple_of(qi * TK, TK)
        for j in range(HB):
            m_prev, l_prev = carry[j]
            p_blocks, m_blocks, l_blocks = [], [], []
            for c in range(TQ // LB):
                lanes = slice(c * LB, (c + 1) * LB)
                sq = jnp.where(diag_ok, s_ref[j, c * LB:(c + 1) * LB, lanes], NEG_BIG)
                if c:
                    st = jnp.concatenate([s_ref[j, 0:c * LB, lanes], sq], axis=0)
                else:
                    st = sq
                m_new = jnp.maximum(m_prev[:, lanes], jnp.max(st, axis=0, keepdims=True))
                pt = jnp.exp2(st - m_new)
                m_blocks.append(m_new)
                l_blocks.append(jnp.sum(pt, axis=0, keepdims=True))
                pt = pt.astype(jnp.bfloat16)
                if (c + 1) * LB < TK:
                    pt = jnp.concatenate(
                        [pt, jnp.zeros((TK - (c + 1) * LB, LB), jnp.bfloat16)], axis=0)
                p_blocks.append(pt)
            m_new = jnp.concatenate(m_blocks, axis=1)
            alpha = jnp.exp2(m_prev - m_new)
            l_new = alpha * l_prev + jnp.concatenate(l_blocks, axis=1)
            pv = _dot(vt_ref[j, :, pl.ds(start, TK)], jnp.concatenate(p_blocks, axis=1))
            park(j, scores(qtn_ref, 0, j))
            ot = (alpha * acc_ref[j] + pv) * (1.0 / l_new)
            o_ref[:, j * DV:(j + 1) * DV] = ot.T.astype(o_ref.dtype)

    init = tuple((jnp.full((1, TQ), NEG_BIG, jnp.float32), jnp.zeros((1, TQ), jnp.float32))
                 for _ in range(HB))
    def tiles(first, count, c):
        for i in range(count):
            c = body(first + i, c)
        return c

    def loop_step(u, c):
        @pl.when((qi == nq - 1) & more_batches & (u == 1))
        def _():
            k_copy(b + 1, 0).start()
            v_copy(b + 1, 0).start()

        return tiles(LOOP_TILES * u, LOOP_TILES, c)

    carry = lax.fori_loop(0, qi // LOOP_TILES, loop_step, init)

    @pl.when((qi == nq - 1) & more_batches)
    def _():
        k_copy(b + 1, 0).wait()

    assert LOOP_TILES == 4
    rem = qi % LOOP_TILES

    def finish(count, c):
        diagonal(tiles(qi - rem, count, c))
        return 0

    lax.cond(rem >= 2,
             lambda c: lax.cond(rem == 3, functools.partial(finish, 3),
                                functools.partial(finish, 2), c),
             lambda c: lax.cond(rem == 1, functools.partial(finish, 1),
                                functools.partial(finish, 0), c),
             carry)


def _attn_call(qt, k, vt, batch, seq):
    assert TQ == TK and HB == HEADS
    nq = seq // TQ

    def next_tile(b, i):
        last = i == nq - 1
        return (jnp.where(last, jnp.minimum(b + 1, batch - 1), b), 0, 0,
                jnp.where(last, 0, i + 1))

    return pl.pallas_call(
        functools.partial(_attn_kernel, nb=batch, nq=nq),
        grid=(batch, nq),
        in_specs=[
            pl.BlockSpec((1, HB, DQK, TQ), lambda b, i: (b, 0, 0, i)),
            pl.BlockSpec((1, HB, DQK, TQ), next_tile),
            pl.BlockSpec(memory_space=pl.ANY),
            pl.BlockSpec(memory_space=pl.ANY),
        ],
        out_specs=pl.BlockSpec((TQ, HB * DV), lambda b, i: (b * nq + i, 0)),
        out_shape=jax.ShapeDtypeStruct((batch * seq, MLA_W), jnp.bfloat16),
        scratch_shapes=[pltpu.VMEM((HB, seq, DQK), jnp.bfloat16),
                        pltpu.VMEM((HB, DV, seq), jnp.bfloat16),
                        pltpu.SemaphoreType.DMA((KV_CHUNKS,)),
                        pltpu.SemaphoreType.DMA((KV_CHUNKS,)),
                        pltpu.VMEM((HB, TK, TQ), jnp.float32),
                        pltpu.VMEM((HB, 1, TQ), jnp.float32),
                        pltpu.VMEM((HB, DV, TQ), jnp.float32)],
        compiler_params=pltpu.CompilerParams(
            dimension_semantics=("arbitrary", "arbitrary"),
            vmem_limit_bytes=VMEM_LIMIT),
        name="mla_attn",
    )(qt, qt, k, vt)


def _out_kernel(x_ref, o_ref, gate_ref, og_ref, omg_ref, wo_ref, fg_ref, out_ref):
    for r0 in range(0, TM_OUT, SUB_OUT):
        rows = slice(r0, r0 + SUB_OUT)
        h = x_ref[rows, :] + _dot(og_ref[rows, :], wo_ref[MLA_W:, :])
        om = _rms(o_ref[rows, :].astype(jnp.float32), omg_ref[...]) \
            * gate_ref[rows, :].astype(jnp.float32)
        h = h + _dot(om.astype(jnp.bfloat16), wo_ref[0:MLA_W, :])
        out_ref[rows, :] = _rms(h, fg_ref[...])


def _out_call(x2, o_mla, gate, og, omg, wo, fg):
    n = x2.shape[0]
    tm = TM_OUT
    row_spec = pl.BlockSpec((tm, D_MODEL), lambda i: (i, 0))
    return pl.pallas_call(
        _out_kernel,
        grid=(n // tm,),
        in_specs=[row_spec, row_spec, row_spec, row_spec,
                  _const_spec(omg.shape), _const_spec(wo.shape), _const_spec(fg.shape)],
        out_specs=row_spec,
        out_shape=jax.ShapeDtypeStruct((n, D_MODEL), jnp.float32),
        compiler_params=pltpu.CompilerParams(
            dimension_semantics=("parallel",), vmem_limit_bytes=VMEM_LIMIT),
        name="out_proj",
    )(x2, o_mla, gate, og, omg, wo, fg)


def _rope_tables(seq):
    inv_freq = ROPE_THETA ** (-jnp.arange(0, ROPE, 2, dtype=jnp.float32) / ROPE)
    hi = jnp.arange(0, seq, POS_SPLIT, dtype=jnp.float32)[:, None] * inv_freq[None, :]
    lo = jnp.arange(POS_SPLIT, dtype=jnp.float32)[:, None] * inv_freq[None, :]
    ch, sh = jnp.cos(hi)[:, None, :], jnp.sin(hi)[:, None, :]
    cl, sl = jnp.cos(lo)[None, :, :], jnp.sin(lo)[None, :, :]
    cos = (ch * cl - sh * sl).reshape(seq, ROPE // 2)
    sin = (sh * cl + ch * sl).reshape(seq, ROPE // 2)
    ones = jnp.ones((NOPE, seq), jnp.float32)
    zeros_n = jnp.zeros((NOPE, seq), jnp.float32)
    pad = jnp.zeros((DQK - NOPE - ROPE, seq), jnp.float32)
    cos_t = jnp.concatenate([ones, cos.T, cos.T, pad], axis=0)
    sin_t = jnp.concatenate([zeros_n, sin.T, sin.T, pad], axis=0)
    return cos_t, sin_t


def _rot_half_cols(w):
    half = ROPE // 2
    return jnp.concatenate([-w[..., half:], w[..., :half]], axis=-1)


def _pack_big_kernel(wt_ref, wbig_ref):
    wbig_ref[...] = wt_ref[...].T.astype(jnp.bfloat16)


def _pack_lat_kernel(wt_ref, wlat_ref):
    w = wt_ref[...]
    kr = w[L_KR:L_KR + ROPE]
    half = ROPE // 2
    blk = jnp.concatenate([w[:L_KR], jnp.zeros((NOPE, w.shape[1]), w.dtype), kr,
                           -kr[half:], kr[:half]], axis=0)
    wlat_ref[...] = blk.T.astype(jnp.bfloat16)


def _pack_w_in(w_in):
    d, d_in = w_in.shape
    wt = w_in.T
    n_lat = L_KR + ROPE
    wbig = pl.pallas_call(
        _pack_big_kernel,
        grid=(D_BIG // PACK_COLS,),
        in_specs=[pl.BlockSpec((pl.Element(PACK_COLS), pl.Element(d)),
                               lambda j: (pl.multiple_of(n_lat + j * PACK_COLS, ROPE), 0))],
        out_specs=pl.BlockSpec((d, PACK_COLS), lambda j: (0, j)),
        out_shape=jax.ShapeDtypeStruct((d, D_BIG), jnp.bfloat16),
        compiler_params=pltpu.CompilerParams(dimension_semantics=("parallel",)),
        name="pack_w_big",
    )(wt)
    wlat = pl.pallas_call(
        _pack_lat_kernel,
        grid=(1,),
        in_specs=[pl.BlockSpec((n_lat, d), lambda j: (0, 0))],
        out_specs=pl.BlockSpec((d, D_LAT), lambda j: (0, 0)),
        out_shape=jax.ShapeDtypeStruct((d, D_LAT), jnp.bfloat16),
        name="pack_w_lat",
    )(wt)
    return wlat, wbig


def _pack_weights(w_in, w_uq, w_ukv):
    bf = jnp.bfloat16
    wlat, wbig = _pack_w_in(w_in)
    wq = w_uq.reshape(Q_RANK, HEADS, NOPE + ROPE)
    wuq = jnp.concatenate([wq, _rot_half_cols(wq[..., NOPE:])],
                          axis=-1).reshape(Q_RANK, HEADS * DQK).T.astype(bf)
    wkv = w_ukv.reshape(KV_RANK, HEADS, NOPE + DV)
    wkn = jnp.concatenate([wkv[..., :NOPE], jnp.zeros((KV_RANK, HEADS, DQK - NOPE), w_ukv.dtype)],
                          axis=-1).reshape(KV_RANK, HEADS * DQK).astype(bf)
    wv = wkv[..., NOPE:].reshape(KV_RANK, HEADS * DV).T.astype(bf)
    return wlat, wbig, wuq, wkn, wv


def kernel(x, norm_in_g, w_in, q_norm_g, w_uq, kv_norm_g, w_ukv, gmlp_ln_g, gmlp_ln_b,
           w_spatial, b_spatial, out_norm_mla_g, out_norm_gmlp_g, w_out, final_norm_g):
    batch, seq, d = x.shape
    n = batch * seq
    x2 = x.reshape(n, d)
    row = lambda a: a.reshape(1, -1).astype(jnp.float32)

    wlat, wbig, wuq, wkn, wv = _pack_weights(w_in, w_uq, w_ukv)
    ws = w_spatial.astype(jnp.float32)
    bs = jnp.broadcast_to(b_spatial[:, :, None], (GROUPS, SBLOCK, GDIM)).astype(jnp.float32)
    consts = (row(norm_in_g), wlat, wbig, row(q_norm_g), wuq, row(kv_norm_g), wkn, wv,
              row(gmlp_ln_g), row(gmlp_ln_b), ws, bs, row(out_norm_gmlp_g))

    qt, k, vt, gate, og = _proj_call(x2, _rope_tables(seq), consts, batch, seq)
    o_mla = _attn_call(qt, k, vt, batch, seq)
    out = _out_call(x2, o_mla, gate, og, row(out_norm_mla_g), w_out.astype(jnp.bfloat16),
                    row(final_norm_g))
    return out.reshape(batch, seq, d)
```

```python
import functools
import math

import jax
import jax.numpy as jnp
from jax import lax
from jax.experimental import pallas as pl
from jax.experimental.pallas import tpu as pltpu

D_MODEL = 1024
CHUNK = 64
EPS = 1e-6

HEADS = 8
NOPE = 64
ROPE = 32
DV = 128
DQK = 128
Q_RANK = 384
KV_RANK = 256
MLA_W = HEADS * DV
ROPE_THETA = 10000.0
QK_SCALE = math.log2(math.e) / math.sqrt(NOPE + ROPE)

GROUPS = 8
GDIM = 128
GMLP_W = GROUPS * GDIM
SBLOCK = 128

L_QLAT = 0
L_KVLAT = L_QLAT + Q_RANK
L_KR = L_KVLAT + KV_RANK
D_LAT = L_KR + DQK
B_GMLA = 0
B_U = B_GMLA + MLA_W
B_V = B_U + GMLP_W
B_GG = B_V + GMLP_W
D_BIG = B_GG + GMLP_W

VMEM_LIMIT = 56 * 1024 * 1024

TM_PROJ = 512
SUB_PROJ = 256
TQ = 512
TK = 512
HB = 8
KV_CHUNKS = 4
LB = 128
TM_OUT = 1024
PACK_COLS = 512
POS_SPLIT = 128
SUB_OUT = 256

NEG_BIG = -1e30


def _rms(x, g, eps=EPS):
    return x * lax.rsqrt(jnp.mean(x * x, axis=-1, keepdims=True) + eps) * g


def _dot(a, b):
    return jnp.dot(a, b, preferred_element_type=jnp.float32)


def _silu(x):
    return x * (1.0 / (1.0 + jnp.exp2(x * (-math.log2(math.e)))))


def _gelu_x2(x):
    return x * (1.0 + lax.erf(x * (1.0 / math.sqrt(2.0))))


def _rope(x, cos, sin):
    return x * cos + pltpu.roll(x, DQK - ROPE, 1) * sin


def _proj_kernel(x_ref, cost_ref, sint_ref, ng_ref, wlat_ref, wbig_ref,
                 qg_ref, wuqt_ref, kvg_ref, wkn_ref, wvt_ref, lng_ref, lnb_ref, ws_ref, bs_ref,
                 ogg_ref, qt_ref, k_ref, vt_ref, gate_ref, og_ref):
    tc = lax.broadcasted_iota(jnp.int32, (SBLOCK, SBLOCK), 0) // CHUNK
    sc = lax.broadcasted_iota(jnp.int32, (SBLOCK, SBLOCK), 1) // CHUNK
    ws = [jnp.where(sc <= tc, ws_ref[g], 0.0).astype(jnp.bfloat16) for g in range(GROUPS)]
    nblk = SUB_PROJ // SBLOCK

    for r0 in range(0, TM_PROJ, SUB_PROJ):
        rows = slice(r0, r0 + SUB_PROJ)
        y = _rms(x_ref[rows, :], ng_ref[...]).astype(jnp.bfloat16)

        v = _gelu_x2(_dot(y, wbig_ref[:, B_V:B_GG]))
        u = _gelu_x2(_dot(y, wbig_ref[:, B_U:B_V]))
        mu = jnp.mean(v, axis=-1, keepdims=True)
        vc = v - mu
        var = jnp.mean(vc * vc, axis=-1, keepdims=True)
        vln = (vc * lax.rsqrt(var + 4.0 * EPS) * lng_ref[...] + lnb_ref[...]).astype(jnp.bfloat16)

        lat = _dot(y, wlat_ref[...])
        gate_ref[rows, :] = _silu(_dot(y, wbig_ref[:, B_GMLA:B_U])).astype(jnp.bfloat16)
        ggs = _silu(_dot(y, wbig_ref[:, B_GG:D_BIG]))

        mixed_g = []
        for g in range(GROUPS):
            cols = slice(g * GDIM, (g + 1) * GDIM)
            vb = jnp.concatenate([vln[c * SBLOCK:(c + 1) * SBLOCK, cols] for c in range(nblk)],
                                 axis=1)
            mixed_g.append(_dot(ws[g], vb))
        mixed = jnp.concatenate(
            [jnp.concatenate([mixed_g[g][:, c * GDIM:(c + 1) * GDIM] + bs_ref[g]
                              for g in range(GROUPS)], axis=1) for c in range(nblk)], axis=0)
        og_ref[rows, :] = (_rms(u * mixed, ogg_ref[...], 4.0 * EPS) * ggs).astype(jnp.bfloat16)

        qn = _rms(lat[:, L_QLAT:L_KVLAT], qg_ref[...])
        kvn = _rms(lat[:, L_KVLAT:L_KR], kvg_ref[...])
        cos_t = cost_ref[:, rows]
        sin_t = sint_ref[:, rows]
        kr = _rope(lat[:, L_KR:D_LAT], cos_t.T, sin_t.T)
        qt_all = _dot(wuqt_ref[...], qn.T.astype(jnp.bfloat16))
        k_all = _dot(kvn.astype(jnp.bfloat16), wkn_ref[...])
        vt_all = _dot(wvt_ref[...], kvn.T.astype(jnp.bfloat16))
        cosq = cos_t * QK_SCALE
        sinq = sin_t * QK_SCALE
        heads = [slice(h * DQK, (h + 1) * DQK) for h in range(HEADS)]
        for h, sl in enumerate(heads):
            qb = qt_all[sl, :]
            rot = jnp.concatenate([qb[:NOPE], qb[NOPE + ROPE:], qb[NOPE + ROPE:]], axis=0)
            qt_ref[0, h, :, rows] = (qb * cosq + rot * sinq).astype(jnp.bfloat16)
        for h, sl in enumerate(heads):
            k_ref[0, h, rows, :] = (k_all[:, sl] + kr).astype(jnp.bfloat16)
        for h, sl in enumerate(heads):
            vt_ref[0, h, :, rows] = vt_all[sl, :].astype(jnp.bfloat16)


def _const_spec(shape):
    nd = len(shape)
    return pl.BlockSpec(shape, lambda i, _nd=nd: (0,) * _nd, pipeline_mode=pl.Buffered(1))


def _proj_call(x2, tables, consts, batch, seq):
    n = x2.shape[0]
    tm = TM_PROJ
    tiles_per_seq = seq // tm
    row_spec = pl.BlockSpec((tm, D_MODEL), lambda i: (i, 0))
    tab_spec = pl.BlockSpec((DQK, tm), lambda i: (0, i % tiles_per_seq))
    head_spec = pl.BlockSpec((1, HEADS, tm, DQK),
                             lambda i: (i // tiles_per_seq, 0, i % tiles_per_seq, 0))
    head_shape = jax.ShapeDtypeStruct((batch, HEADS, seq, DQK), jnp.bfloat16)
    headt_spec = pl.BlockSpec((1, HEADS, DQK, tm),
                              lambda i: (i // tiles_per_seq, 0, 0, i % tiles_per_seq))
    headt_shape = jax.ShapeDtypeStruct((batch, HEADS, DQK, seq), jnp.bfloat16)
    row_shape = jax.ShapeDtypeStruct((n, D_MODEL), jnp.bfloat16)
    return pl.pallas_call(
        _proj_kernel,
        grid=(n // tm,),
        in_specs=[row_spec] + [tab_spec] * len(tables) + [_const_spec(c.shape) for c in consts],
        out_specs=[headt_spec, head_spec, headt_spec, row_spec, row_spec],
        out_shape=[headt_shape, head_shape, headt_shape, row_shape, row_shape],
        compiler_params=pltpu.CompilerParams(
            dimension_semantics=("parallel",), vmem_limit_bytes=VMEM_LIMIT),
        name="proj_gmlp",
    )(x2, *tables, *consts)


def _attn_kernel(qt_ref, qtn_ref, k_hbm, vt_hbm, o_ref, k_ref, vt_ref, ksem, vsem,
                 s_ref, cm_ref, acc_ref, *, nb, nq):
    b = pl.program_id(0)
    qi = pl.program_id(1)
    chunk = k_ref.shape[1] // KV_CHUNKS
    assert chunk == 4 * TK and (nq - 1) // 4 >= 2
    more_batches = b < nb - 1
    acc_ref[...] = jnp.zeros_like(acc_ref)

    def k_copy(bb, c):
        rows = pl.ds(c * chunk, chunk)
        return pltpu.make_async_copy(k_hbm.at[bb, :, rows, :], k_ref.at[:, rows, :], ksem.at[c])

    def v_copy(bb, c):
        cols = pl.ds(c * chunk, chunk)
        return pltpu.make_async_copy(vt_hbm.at[bb, :, :, cols], vt_ref.at[:, :, cols],
                                     vsem.at[c])

    @pl.when(qi == 0)
    def _():
        @pl.when(b == 0)
        def _():
            k_copy(0, 0).start()
            v_copy(0, 0).start()
            k_copy(0, 0).wait()
        v_copy(b, 0).wait()
        for c in range(1, KV_CHUNKS):
            k_copy(b, c).start()
            v_copy(b, c).start()

    for c in range(1, KV_CHUNKS):
        @pl.when(qi == c * (chunk // TK))
        def _(c=c):
            k_copy(b, c).wait()
            v_copy(b, c).wait()

    def scores(q_ref, t, j):
        start = pl.multiple_of(t * TK, TK)
        return _dot(k_ref[j, pl.ds(start, TK), :], q_ref[0, j])

    def park(j, st):
        s_ref[j] = st
        cm_ref[j] = jnp.max(st, axis=0, keepdims=True)

    def consume(t, j, st, state, cmax):
        m_prev, l_prev = state
        start = pl.multiple_of(t * TK, TK)
        m_new = jnp.maximum(m_prev, cmax)
        alpha = jnp.exp2(m_prev - m_new)
        pt = jnp.exp2(st - m_new)
        l_new = alpha * l_prev + jnp.sum(pt, axis=0, keepdims=True)
        vt = vt_ref[j, :, pl.ds(start, TK)]
        return m_new, l_new, alpha, (vt, pt.astype(jnp.bfloat16))

    @pl.when((qi == 0) & (b == 0))
    def _():
        for j in range(HB):
            park(j, scores(qt_ref, 0, j))

    def body(t, carry):
        out = []
        for j in range(HB):
            m_prev, l_prev = carry[j]
            m_new, l_new, alpha, (vt, pt) = consume(t, j, s_ref[j], (m_prev, l_prev), cm_ref[j])
            park(j, scores(qt_ref, t + 1, j))
            acc_ref[j] = alpha * acc_ref[j] + _dot(vt, pt)
            out.append((m_new, l_new))
        return tuple(out)

    def diagonal(carry):
        kc = lax.broadcasted_iota(jnp.int32, (LB, LB), 0) // CHUNK
        qc = lax.broadcasted_iota(jnp.int32, (LB, LB), 1) // CHUNK
        diag_ok = kc <= qc
        start = pl.multiple_of(qi * TK, TK)
        for j in range(HB):
            m_prev, l_prev = carry[j]
            p_blocks, m_blocks, l_blocks = [], [], []
            for c in range(TQ // LB):
                lanes = slice(c * LB, (c + 1) * LB)
                sq = jnp.where(diag_ok, s_ref[j, c * LB:(c + 1) * LB, lanes], NEG_BIG)
                if c:
                    st = jnp.concatenate([s_ref[j, 0:c * LB, lanes], sq], axis=0)
                else:
                    st = sq
                m_new = jnp.maximum(m_prev[:, lanes], jnp.max(st, axis=0, keepdims=True))
                pt = jnp.exp2(st - m_new)
                m_blocks.append(m_new)
                l_blocks.append(jnp.sum(pt, axis=0, keepdims=True))
                pt = pt.astype(jnp.bfloat16)
                if (c + 1) * LB < TK:
                    pt = jnp.concatenate(
                        [pt, jnp.zeros((TK - (c + 1) * LB, LB), jnp.bfloat16)], axis=0)
                p_blocks.append(pt)
            m_new = jnp.concatenate(m_blocks, axis=1)
            alpha = jnp.exp2(m_prev - m_new)
            l_new = alpha * l_prev + jnp.concatenate(l_blocks, axis=1)
            pv = _dot(vt_ref[j, :, pl.ds(start, TK)], jnp.concatenate(p_blocks, axis=1))
            park(j, scores(qtn_ref, 0, j))
            ot = (alpha * acc_ref[j] + pv) * (1.0 / l_new)
            o_ref[:, j * DV:(j + 1) * DV] = ot.T.astype(o_ref.dtype)

    init = tuple((jnp.full((1, TQ), NEG_BIG, jnp.float32), jnp.zeros((1, TQ), jnp.float32))
                 for _ in range(HB))
    def tiles(first, count, c):
        for i in range(count):
            c = body(first + i, c)
        return c

    def loop_step(u, c):
        @pl.when((qi == nq - 1) & more_batches & (u == 1))
        def _():
            k_copy(b + 1, 0).start()
            v_copy(b + 1, 0).start()

        return tiles(4 * u, 4, c)

    carry = lax.fori_loop(0, qi // 4, loop_step, init)

    @pl.when((qi == nq - 1) & more_batches)
    def _():
        k_copy(b + 1, 0).wait()

    rem = qi % 4

    def finish(count, c):
        diagonal(tiles((qi // 4) * 4, count, c))
        return 0

    lax.cond(rem >= 2,
             lambda c: lax.cond(rem == 3, functools.partial(finish, 3),
                                functools.partial(finish, 2), c),
             lambda c: lax.cond(rem == 1, functools.partial(finish, 1),
                                functools.partial(finish, 0), c),
             carry)


def _attn_call(qt, k, vt, batch, seq):
    assert TQ == TK and HB == HEADS
    nq = seq // TQ

    def next_tile(b, i):
        last = i == nq - 1
        return (jnp.where(last, jnp.minimum(b + 1, batch - 1), b), 0, 0,
                jnp.where(last, 0, i + 1))

    return pl.pallas_call(
        functools.partial(_attn_kernel, nb=batch, nq=nq),
        grid=(batch, nq),
        in_specs=[
            pl.BlockSpec((1, HB, DQK, TQ), lambda b, i: (b, 0, 0, i)),
            pl.BlockSpec((1, HB, DQK, TQ), next_tile),
            pl.BlockSpec(memory_space=pl.ANY),
            pl.BlockSpec(memory_space=pl.ANY),
        ],
        out_specs=pl.BlockSpec((TQ, HB * DV), lambda b, i: (b * nq + i, 0)),
        out_shape=jax.ShapeDtypeStruct((batch * seq, MLA_W), jnp.bfloat16),
        scratch_shapes=[pltpu.VMEM((HB, seq, DQK), jnp.bfloat16),
                        pltpu.VMEM((HB, DV, seq), jnp.bfloat16),
                        pltpu.SemaphoreType.DMA((KV_CHUNKS,)),
                        pltpu.SemaphoreType.DMA((KV_CHUNKS,)),
                        pltpu.VMEM((HB, TK, TQ), jnp.float32),
                        pltpu.VMEM((HB, 1, TQ), jnp.float32),
                        pltpu.VMEM((HB, DV, TQ), jnp.float32)],
        compiler_params=pltpu.CompilerParams(
            dimension_semantics=("arbitrary", "arbitrary"),
            vmem_limit_bytes=VMEM_LIMIT),
        name="mla_attn",
    )(qt, qt, k, vt)


def _out_kernel(x_ref, o_ref, gate_ref, og_ref, omg_ref, wo_ref, fg_ref, out_ref):
    for r0 in range(0, TM_OUT, SUB_OUT):
        rows = slice(r0, r0 + SUB_OUT)
        h = x_ref[rows, :] + _dot(og_ref[rows, :], wo_ref[MLA_W:, :])
        om = _rms(o_ref[rows, :].astype(jnp.float32), omg_ref[...]) \
            * gate_ref[rows, :].astype(jnp.float32)
        h = h + _dot(om.astype(jnp.bfloat16), wo_ref[0:MLA_W, :])
        out_ref[rows, :] = _rms(h, fg_ref[...])


def _out_call(x2, o_mla, gate, og, omg, wo, fg):
    n = x2.shape[0]
    tm = TM_OUT
    row_spec = pl.BlockSpec((tm, D_MODEL), lambda i: (i, 0))
    return pl.pallas_call(
        _out_kernel,
        grid=(n // tm,),
        in_specs=[row_spec, row_spec, row_spec, row_spec,
                  _const_spec(omg.shape), _const_spec(wo.shape), _const_spec(fg.shape)],
        out_specs=row_spec,
        out_shape=jax.ShapeDtypeStruct((n, D_MODEL), jnp.float32),
        compiler_params=pltpu.CompilerParams(
            dimension_semantics=("parallel",), vmem_limit_bytes=VMEM_LIMIT),
        name="out_proj",
    )(x2, o_mla, gate, og, omg, wo, fg)


def _rope_tables(seq):
    inv_freq = ROPE_THETA ** (-jnp.arange(0, ROPE, 2, dtype=jnp.float32) / ROPE)
    hi = jnp.arange(0, seq, POS_SPLIT, dtype=jnp.float32)[:, None] * inv_freq[None, :]
    lo = jnp.arange(POS_SPLIT, dtype=jnp.float32)[:, None] * inv_freq[None, :]
    ch, sh = jnp.cos(hi)[:, None, :], jnp.sin(hi)[:, None, :]
    cl, sl = jnp.cos(lo)[None, :, :], jnp.sin(lo)[None, :, :]
    cos = (ch * cl - sh * sl).reshape(seq, ROPE // 2)
    sin = (sh * cl + ch * sl).reshape(seq, ROPE // 2)
    ones = jnp.ones((NOPE, seq), jnp.float32)
    zeros_n = jnp.zeros((NOPE, seq), jnp.float32)
    pad = jnp.zeros((DQK - NOPE - ROPE, seq), jnp.float32)
    cos_t = jnp.concatenate([ones, cos.T, cos.T, pad], axis=0)
    sin_t = jnp.concatenate([zeros_n, sin.T, sin.T, pad], axis=0)
    return cos_t, sin_t


def _rot_half_cols(w):
    half = ROPE // 2
    return jnp.concatenate([-w[..., half:], w[..., :half]], axis=-1)


def _pack_big_kernel(wt_ref, wbig_ref):
    wbig_ref[...] = wt_ref[...].T.astype(jnp.bfloat16)


def _pack_lat_kernel(wt_ref, wlat_ref):
    w = wt_ref[...]
    kr = w[L_KR:L_KR + ROPE]
    half = ROPE // 2
    blk = jnp.concatenate([w[:L_KR], jnp.zeros((NOPE, w.shape[1]), w.dtype), kr,
                           -kr[half:], kr[:half]], axis=0)
    wlat_ref[...] = blk.T.astype(jnp.bfloat16)


def _pack_w_in(w_in):
    d, d_in = w_in.shape
    wt = w_in.T
    n_lat = L_KR + ROPE
    wbig = pl.pallas_call(
        _pack_big_kernel,
        grid=(D_BIG // PACK_COLS,),
        in_specs=[pl.BlockSpec((pl.Element(PACK_COLS), pl.Element(d)),
                               lambda j: (pl.multiple_of(n_lat + j * PACK_COLS, ROPE), 0))],
        out_specs=pl.BlockSpec((d, PACK_COLS), lambda j: (0, j)),
        out_shape=jax.ShapeDtypeStruct((d, D_BIG), jnp.bfloat16),
        compiler_params=pltpu.CompilerParams(dimension_semantics=("parallel",)),
        name="pack_w_big",
    )(wt)
    wlat = pl.pallas_call(
        _pack_lat_kernel,
        grid=(1,),
        in_specs=[pl.BlockSpec((n_lat, d), lambda j: (0, 0))],
        out_specs=pl.BlockSpec((d, D_LAT), lambda j: (0, 0)),
        out_shape=jax.ShapeDtypeStruct((d, D_LAT), jnp.bfloat16),
        name="pack_w_lat",
    )(wt)
    return wlat, wbig


def _pack_weights(w_in, w_uq, w_ukv):
    bf = jnp.bfloat16
    wlat, wbig = _pack_w_in(w_in)
    wq = w_uq.reshape(Q_RANK, HEADS, NOPE + ROPE)
    wuq = jnp.concatenate([wq, _rot_half_cols(wq[..., NOPE:])],
                          axis=-1).reshape(Q_RANK, HEADS * DQK).T.astype(bf)
    wkv = w_ukv.reshape(KV_RANK, HEADS, NOPE + DV)
    wkn = jnp.concatenate([wkv[..., :NOPE], jnp.zeros((KV_RANK, HEADS, DQK - NOPE), w_ukv.dtype)],
                          axis=-1).reshape(KV_RANK, HEADS * DQK).astype(bf)
    wv = wkv[..., NOPE:].reshape(KV_RANK, HEADS * DV).T.astype(bf)
    return wlat, wbig, wuq, wkn, wv


def kernel(x, norm_in_g, w_in, q_norm_g, w_uq, kv_norm_g, w_ukv, gmlp_ln_g, gmlp_ln_b,
           w_spatial, b_spatial, out_norm_mla_g, out_norm_gmlp_g, w_out, final_norm_g):
    batch, seq, d = x.shape
    n = batch * seq
    x2 = x.reshape(n, d)
    row = lambda a: a.reshape(1, -1).astype(jnp.float32)

    wlat, wbig, wuq, wkn, wv = _pack_weights(w_in, w_uq, w_ukv)
    ws = w_spatial.astype(jnp.float32)
    bs = jnp.broadcast_to(b_spatial[:, :, None], (GROUPS, SBLOCK, GDIM)).astype(jnp.float32)
    consts = (row(norm_in_g), wlat, wbig, row(q_norm_g), wuq, row(kv_norm_g), wkn, wv,
              row(gmlp_ln_g), row(gmlp_ln_b), ws, bs, row(out_norm_gmlp_g))

    qt, k, vt, gate, og = _proj_call(x2, _rope_tables(seq), consts, batch, seq)
    o_mla = _attn_call(qt, k, vt, batch, seq)
    out = _out_call(x2, o_mla, gate, og, row(out_norm_mla_g), w_out.astype(jnp.bfloat16),
                    row(final_norm_g))
    return out.reshape(batch, seq, d)
```

```python
import functools
import math

import jax
import jax.numpy as jnp
from jax import lax
from jax.experimental import pallas as pl
from jax.experimental.pallas import tpu as pltpu

D_MODEL = 1024
CHUNK = 64
EPS = 1e-6

HEADS = 8
NOPE = 64
ROPE = 32
DV = 128
DQK = 128
Q_RANK = 384
KV_RANK = 256
MLA_W = HEADS * DV
ROPE_THETA = 10000.0
QK_SCALE = math.log2(math.e) / math.sqrt(NOPE + ROPE)

GROUPS = 8
GDIM = 128
GMLP_W = GROUPS * GDIM
SBLOCK = 128

L_QLAT = 0
L_KVLAT = L_QLAT + Q_RANK
L_KR = L_KVLAT + KV_RANK
D_LAT = L_KR + DQK
B_GMLA = 0
B_U = B_GMLA + MLA_W
B_V = B_U + GMLP_W
B_GG = B_V + GMLP_W
D_BIG = B_GG + GMLP_W

VMEM_LIMIT = 56 * 1024 * 1024

TM_PROJ = 512
SUB_PROJ = 256
TQ = 512
TK = 512
HB = 8
KV_CHUNKS = 4
LOOP_TILES = 4
LB = 128
TM_OUT = 1024
PACK_COLS = 1024
POS_SPLIT = 128
SUB_OUT = 256

NEG_BIG = -1e30


def _rms(x, g, eps=EPS):
    return x * lax.rsqrt(jnp.mean(x * x, axis=-1, keepdims=True) + eps) * g


def _dot(a, b):
    return jnp.dot(a, b, preferred_element_type=jnp.float32)


def _silu(x):
    return x * (1.0 / (1.0 + jnp.exp2(x * (-math.log2(math.e)))))


def _gelu_x2(x):
    return x * (1.0 + lax.erf(x * (1.0 / math.sqrt(2.0))))


def _rope(x, cos, sin):
    return x * cos + pltpu.roll(x, DQK - ROPE, 1) * sin


def _proj_kernel(x_ref, cost_ref, sint_ref, ng_ref, wlat_ref, wbig_ref,
                 qg_ref, wuqt_ref, kvg_ref, wkn_ref, wvt_ref, lng_ref, lnb_ref, ws_ref, bs_ref,
                 ogg_ref, qt_ref, k_ref, vt_ref, gate_ref, og_ref):
    tc = lax.broadcasted_iota(jnp.int32, (SBLOCK, SBLOCK), 0) // CHUNK
    sc = lax.broadcasted_iota(jnp.int32, (SBLOCK, SBLOCK), 1) // CHUNK
    ws = [jnp.where(sc <= tc, ws_ref[g], 0.0).astype(jnp.bfloat16) for g in range(GROUPS)]
    nblk = SUB_PROJ // SBLOCK

    for r0 in range(0, TM_PROJ, SUB_PROJ):
        rows = slice(r0, r0 + SUB_PROJ)
        y = _rms(x_ref[rows, :], ng_ref[...]).astype(jnp.bfloat16)

        v = _gelu_x2(_dot(y, wbig_ref[:, B_V:B_GG]))
        u = _gelu_x2(_dot(y, wbig_ref[:, B_U:B_V]))
        mu = jnp.mean(v, axis=-1, keepdims=True)
        vc = v - mu
        var = jnp.mean(vc * vc, axis=-1, keepdims=True)
        vln = (vc * lax.rsqrt(var + 4.0 * EPS) * lng_ref[...] + lnb_ref[...]).astype(jnp.bfloat16)

        lat = _dot(y, wlat_ref[...])
        gate_ref[rows, :] = _silu(_dot(y, wbig_ref[:, B_GMLA:B_U])).astype(jnp.bfloat16)
        ggs = _silu(_dot(y, wbig_ref[:, B_GG:D_BIG]))

        mixed_g = []
        for g in range(GROUPS):
            cols = slice(g * GDIM, (g + 1) * GDIM)
            vb = jnp.concatenate([vln[c * SBLOCK:(c + 1) * SBLOCK, cols] for c in range(nblk)],
                                 axis=1)
            mixed_g.append(_dot(ws[g], vb))
        mixed = jnp.concatenate(
            [jnp.concatenate([mixed_g[g][:, c * GDIM:(c + 1) * GDIM] + bs_ref[g]
                              for g in range(GROUPS)], axis=1) for c in range(nblk)], axis=0)
        og_ref[rows, :] = (_rms(u * mixed, ogg_ref[...], 4.0 * EPS) * ggs).astype(jnp.bfloat16)

        qn = _rms(lat[:, L_QLAT:L_KVLAT], qg_ref[...])
        kvn = _rms(lat[:, L_KVLAT:L_KR], kvg_ref[...])
        cos_t = cost_ref[:, rows]
        sin_t = sint_ref[:, rows]
        kr = _rope(lat[:, L_KR:D_LAT], cos_t.T, sin_t.T)
        qt_all = _dot(wuqt_ref[...], qn.T.astype(jnp.bfloat16))
        k_all = _dot(kvn.astype(jnp.bfloat16), wkn_ref[...])
        vt_all = _dot(wvt_ref[...], kvn.T.astype(jnp.bfloat16))
        cosq = cos_t * QK_SCALE
        sinq = sin_t * QK_SCALE
        heads = [slice(h * DQK, (h + 1) * DQK) for h in range(HEADS)]
        for h, sl in enumerate(heads):
            qb = qt_all[sl, :]
            rot = jnp.concatenate([qb[:NOPE], qb[NOPE + ROPE:], qb[NOPE + ROPE:]], axis=0)
            qt_ref[0, h, :, rows] = (qb * cosq + rot * sinq).astype(jnp.bfloat16)
        for h, sl in enumerate(heads):
            k_ref[0, h, rows, :] = (k_all[:, sl] + kr).astype(jnp.bfloat16)
        for h, sl in enumerate(heads):
            vt_ref[0, h, :, rows] = vt_all[sl, :].astype(jnp.bfloat16)


def _const_spec(shape):
    nd = len(shape)
    return pl.BlockSpec(shape, lambda i, _nd=nd: (0,) * _nd, pipeline_mode=pl.Buffered(1))


def _proj_call(x2, tables, consts, batch, seq):
    n = x2.shape[0]
    tm = TM_PROJ
    tiles_per_seq = seq // tm
    row_spec = pl.BlockSpec((tm, D_MODEL), lambda i: (i, 0))
    tab_spec = pl.BlockSpec((DQK, tm), lambda i: (0, i % tiles_per_seq))
    head_spec = pl.BlockSpec((1, HEADS, tm, DQK),
                             lambda i: (i // tiles_per_seq, 0, i % tiles_per_seq, 0))
    head_shape = jax.ShapeDtypeStruct((batch, HEADS, seq, DQK), jnp.bfloat16)
    headt_spec = pl.BlockSpec((1, HEADS, DQK, tm),
                              lambda i: (i // tiles_per_seq, 0, 0, i % tiles_per_seq))
    headt_shape = jax.ShapeDtypeStruct((batch, HEADS, DQK, seq), jnp.bfloat16)
    row_shape = jax.ShapeDtypeStruct((n, D_MODEL), jnp.bfloat16)
    return pl.pallas_call(
        _proj_kernel,
        grid=(n // tm,),
        in_specs=[row_spec] + [tab_spec] * len(tables) + [_const_spec(c.shape) for c in consts],
        out_specs=[headt_spec, head_spec, headt_spec, row_spec, row_spec],
        out_shape=[headt_shape, head_shape, headt_shape, row_shape, row_shape],
        compiler_params=pltpu.CompilerParams(
            dimension_semantics=("parallel",), vmem_limit_bytes=VMEM_LIMIT),
        name="proj_gmlp",
    )(x2, *tables, *consts)


def _attn_kernel(qt_ref, qtn_ref, k_hbm, vt_hbm, o_ref, k_ref, vt_ref, ksem, vsem,
                 s_ref, cm_ref, acc_ref, *, nb, nq):
    b = pl.program_id(0)
    qi = pl.program_id(1)
    chunk = k_ref.shape[1] // KV_CHUNKS
    assert chunk == LOOP_TILES * TK and (nq - 1) // LOOP_TILES >= 2
    more_batches = b < nb - 1
    acc_ref[...] = jnp.zeros_like(acc_ref)

    def k_copy(bb, c):
        rows = pl.ds(c * chunk, chunk)
        return pltpu.make_async_copy(k_hbm.at[bb, :, rows, :], k_ref.at[:, rows, :], ksem.at[c])

    def v_copy(bb, c):
        cols = pl.ds(c * chunk, chunk)
        return pltpu.make_async_copy(vt_hbm.at[bb, :, :, cols], vt_ref.at[:, :, cols],
                                     vsem.at[c])

    @pl.when(qi == 0)
    def _():
        @pl.when(b == 0)
        def _():
            k_copy(0, 0).start()
            v_copy(0, 0).start()
            k_copy(0, 0).wait()
        v_copy(b, 0).wait()
        for c in range(1, KV_CHUNKS):
            k_copy(b, c).start()
            v_copy(b, c).start()

    for c in range(1, KV_CHUNKS):
        @pl.when(qi == c * (chunk // TK))
        def _(c=c):
            k_copy(b, c).wait()
            v_copy(b, c).wait()

    def scores(q_ref, t, j):
        start = pl.multiple_of(t * TK, TK)
        return _dot(k_ref[j, pl.ds(start, TK), :], q_ref[0, j])

    def park(j, st):
        s_ref[j] = st
        cm_ref[j] = jnp.max(st, axis=0, keepdims=True)

    def consume(t, j, st, state, cmax):
        m_prev, l_prev = state
        start = pl.multiple_of(t * TK, TK)
        m_new = jnp.maximum(m_prev, cmax)
        alpha = jnp.exp2(m_prev - m_new)
        pt = jnp.exp2(st - m_new)
        l_new = alpha * l_prev + jnp.sum(pt, axis=0, keepdims=True)
        vt = vt_ref[j, :, pl.ds(start, TK)]
        return m_new, l_new, alpha, (vt, pt.astype(jnp.bfloat16))

    @pl.when((qi == 0) & (b == 0))
    def _():
        for j in range(HB):
            park(j, scores(qt_ref, 0, j))

    def body(t, carry):
        out = []
        for j in range(HB):
            m_prev, l_prev = carry[j]
            m_new, l_new, alpha, (vt, pt) = consume(t, j, s_ref[j], (m_prev, l_prev), cm_ref[j])
            park(j, scores(qt_ref, t + 1, j))
            acc_ref[j] = alpha * acc_ref[j] + _dot(vt, pt)
            out.append((m_new, l_new))
        return tuple(out)

    def diagonal(carry):
        kc = lax.broadcasted_iota(jnp.int32, (LB, LB), 0) // CHUNK
        qc = lax.broadcasted_iota(jnp.int32, (LB, LB), 1) // CHUNK
        diag_ok = kc <= qc
        start = pl.multiple_of(qi * TK, TK)
        for j in range(HB):
            m_prev, l_prev = carry[j]
            p_blocks, m_blocks, l_blocks = [], [], []
            for c in range(TQ // LB):
                lanes = slice(c * LB, (c + 1) * LB)
                sq = jnp.where(diag_ok, s_ref[j, c * LB:(c + 1) * LB, lanes], NEG_BIG)
                if c:
                    st = jnp.concatenate([s_ref[j, 0:c * LB, lanes], sq], axis=0)
                else:
                    st = sq
                m_new = jnp.maximum(m_prev[:, lanes], jnp.max(st, axis=0, keepdims=True))
                pt = jnp.exp2(st - m_new)
                m_blocks.append(m_new)
                l_blocks.append(jnp.sum(pt, axis=0, keepdims=True))
                pt = pt.astype(jnp.bfloat16)
                if (c + 1) * LB < TK:
                    pt = jnp.concatenate(
                        [pt, jnp.zeros((TK - (c + 1) * LB, LB), jnp.bfloat16)], axis=0)
                p_blocks.append(pt)
            m_new = jnp.concatenate(m_blocks, axis=1)
            alpha = jnp.exp2(m_prev - m_new)
            l_new = alpha * l_prev + jnp.concatenate(l_blocks, axis=1)
            park(j, scores(qtn_ref, 0, j))
            pv = _dot(vt_ref[j, :, pl.ds(start, TK)], jnp.concatenate(p_blocks, axis=1))
            ot = (alpha * acc_ref[j] + pv) * (1.0 / l_new)
            o_ref[:, j * DV:(j + 1) * DV] = ot.T.astype(o_ref.dtype)

    init = tuple((jnp.full((1, TQ), NEG_BIG, jnp.float32), jnp.zeros((1, TQ), jnp.float32))
                 for _ in range(HB))
    def tiles(first, count, c):
        for i in range(count):
            c = body(first + i, c)
        return c

    def loop_step(u, c):
        @pl.when((qi == nq - 1) & more_batches & (u == 1))
        def _():
            k_copy(b + 1, 0).start()
            v_copy(b + 1, 0).start()

        return tiles(LOOP_TILES * u, LOOP_TILES, c)

    carry = lax.fori_loop(0, qi // LOOP_TILES, loop_step, init)

    @pl.when((qi == nq - 1) & more_batches)
    def _():
        k_copy(b + 1, 0).wait()

    assert LOOP_TILES == 4
    rem = qi % LOOP_TILES

    def finish(count, c):
        diagonal(tiles(qi - rem, count, c))
        return 0

    lax.cond(rem >= 2,
             lambda c: lax.cond(rem == 3, functools.partial(finish, 3),
                                functools.partial(finish, 2), c),
             lambda c: lax.cond(rem == 1, functools.partial(finish, 1),
                                functools.partial(finish, 0), c),
             carry)


def _attn_call(qt, k, vt, batch, seq):
    assert TQ == TK and HB == HEADS
    nq = seq // TQ

    def next_tile(b, i):
        last = i == nq - 1
        return (jnp.where(last, jnp.minimum(b + 1, batch - 1), b), 0, 0,
                jnp.where(last, 0, i + 1))

    return pl.pallas_call(
        functools.partial(_attn_kernel, nb=batch, nq=nq),
        grid=(batch, nq),
        in_specs=[
            pl.BlockSpec((1, HB, DQK, TQ), lambda b, i: (b, 0, 0, i)),
            pl.BlockSpec((1, HB, DQK, TQ), next_tile),
            pl.BlockSpec(memory_space=pl.ANY),
            pl.BlockSpec(memory_space=pl.ANY),
        ],
        out_specs=pl.BlockSpec((TQ, HB * DV), lambda b, i: (b * nq + i, 0)),
        out_shape=jax.ShapeDtypeStruct((batch * seq, MLA_W), jnp.bfloat16),
        scratch_shapes=[pltpu.VMEM((HB, seq, DQK), jnp.bfloat16),
                        pltpu.VMEM((HB, DV, seq), jnp.bfloat16),
                        pltpu.SemaphoreType.DMA((KV_CHUNKS,)),
                        pltpu.SemaphoreType.DMA((KV_CHUNKS,)),
                        pltpu.VMEM((HB, TK, TQ), jnp.float32),
                        pltpu.VMEM((HB, 1, TQ), jnp.float32),
                        pltpu.VMEM((HB, DV, TQ), jnp.float32)],
        compiler_params=pltpu.CompilerParams(
            dimension_semantics=("arbitrary", "arbitrary"),
            vmem_limit_bytes=VMEM_LIMIT),
        name="mla_attn",
    )(qt, qt, k, vt)


def _out_kernel(x_ref, o_ref, gate_ref, og_ref, omg_ref, wo_ref, fg_ref, out_ref):
    for r0 in range(0, TM_OUT, SUB_OUT):
        rows = slice(r0, r0 + SUB_OUT)
        h = x_ref[rows, :] + _dot(og_ref[rows, :], wo_ref[MLA_W:, :])
        om = _rms(o_ref[rows, :].astype(jnp.float32), omg_ref[...]) \
            * gate_ref[rows, :].astype(jnp.float32)
        h = h + _dot(om.astype(jnp.bfloat16), wo_ref[0:MLA_W, :])
        out_ref[rows, :] = _rms(h, fg_ref[...])


def _out_call(x2, o_mla, gate, og, omg, wo, fg):
    n = x2.shape[0]
    tm = TM_OUT
    row_spec = pl.BlockSpec((tm, D_MODEL), lambda i: (i, 0))
    return pl.pallas_call(
        _out_kernel,
        grid=(n // tm,),
        in_specs=[row_spec, row_spec, row_spec, row_spec,
                  _const_spec(omg.shape), _const_spec(wo.shape), _const_spec(fg.shape)],
        out_specs=row_spec,
        out_shape=jax.ShapeDtypeStruct((n, D_MODEL), jnp.float32),
        compiler_params=pltpu.CompilerParams(
            dimension_semantics=("parallel",), vmem_limit_bytes=VMEM_LIMIT),
        name="out_proj",
    )(x2, o_mla, gate, og, omg, wo, fg)


def _rope_tables(seq):
    inv_freq = ROPE_THETA ** (-jnp.arange(0, ROPE, 2, dtype=jnp.float32) / ROPE)
    hi = jnp.arange(0, seq, POS_SPLIT, dtype=jnp.float32)[:, None] * inv_freq[None, :]
    lo = jnp.arange(POS_SPLIT, dtype=jnp.float32)[:, None] * inv_freq[None, :]
    ch, sh = jnp.cos(hi)[:, None, :], jnp.sin(hi)[:, None, :]
    cl, sl = jnp.cos(lo)[None, :, :], jnp.sin(lo)[None, :, :]
    cos = (ch * cl - sh * sl).reshape(seq, ROPE // 2)
    sin = (sh * cl + ch * sl).reshape(seq, ROPE // 2)
    ones = jnp.ones((NOPE, seq), jnp.float32)
    zeros_n = jnp.zeros((NOPE, seq), jnp.float32)
    pad = jnp.zeros((DQK - NOPE - ROPE, seq), jnp.float32)
    cos_t = jnp.concatenate([ones, cos.T, cos.T, pad], axis=0)
    sin_t = jnp.concatenate([zeros_n, sin.T, sin.T, pad], axis=0)
    return cos_t, sin_t


def _rot_half_cols(w):
    half = ROPE // 2
    return jnp.concatenate([-w[..., half:], w[..., :half]], axis=-1)


def _pack_big_kernel(wt_ref, wbig_ref):
    wbig_ref[...] = wt_ref[...].T.astype(jnp.bfloat16)


def _pack_lat_kernel(wt_ref, wlat_ref):
    w = wt_ref[...]
    kr = w[L_KR:L_KR + ROPE]
    half = ROPE // 2
    blk = jnp.concatenate([w[:L_KR], jnp.zeros((NOPE, w.shape[1]), w.dtype), kr,
                           -kr[half:], kr[:half]], axis=0)
    wlat_ref[...] = blk.T.astype(jnp.bfloat16)


def _pack_w_in(w_in):
    d, d_in = w_in.shape
    wt = w_in.T
    n_lat = L_KR + ROPE
    wbig = pl.pallas_call(
        _pack_big_kernel,
        grid=(D_BIG // PACK_COLS,),
        in_specs=[pl.BlockSpec((pl.Element(PACK_COLS), pl.Element(d)),
                               lambda j: (pl.multiple_of(n_lat + j * PACK_COLS, ROPE), 0))],
        out_specs=pl.BlockSpec((d, PACK_COLS), lambda j: (0, j)),
        out_shape=jax.ShapeDtypeStruct((d, D_BIG), jnp.bfloat16),
        compiler_params=pltpu.CompilerParams(dimension_semantics=("parallel",)),
        name="pack_w_big",
    )(wt)
    wlat = pl.pallas_call(
        _pack_lat_kernel,
        grid=(1,),
        in_specs=[pl.BlockSpec((n_lat, d), lambda j: (0, 0))],
        out_specs=pl.BlockSpec((d, D_LAT), lambda j: (0, 0)),
        out_shape=jax.ShapeDtypeStruct((d, D_LAT), jnp.bfloat16),
        name="pack_w_lat",
    )(wt)
    return wlat, wbig


def _pack_weights(w_in, w_uq, w_ukv):
    bf = jnp.bfloat16
    wlat, wbig = _pack_w_in(w_in)
    wq = w_uq.reshape(Q_RANK, HEADS, NOPE + ROPE)
    wuq = jnp.concatenate([wq, _rot_half_cols(wq[..., NOPE:])],
                          axis=-1).reshape(Q_RANK, HEADS * DQK).T.astype(bf)
    wkv = w_ukv.reshape(KV_RANK, HEADS, NOPE + DV)
    wkn = jnp.concatenate([wkv[..., :NOPE], jnp.zeros((KV_RANK, HEADS, DQK - NOPE), w_ukv.dtype)],
                          axis=-1).reshape(KV_RANK, HEADS * DQK).astype(bf)
    wv = wkv[..., NOPE:].reshape(KV_RANK, HEADS * DV).T.astype(bf)
    return wlat, wbig, wuq, wkn, wv


def kernel(x, norm_in_g, w_in, q_norm_g, w_uq, kv_norm_g, w_ukv, gmlp_ln_g, gmlp_ln_b,
           w_spatial, b_spatial, out_norm_mla_g, out_norm_gmlp_g, w_out, final_norm_g):
    batch, seq, d = x.shape
    n = batch * seq
    x2 = x.reshape(n, d)
    row = lambda a: a.reshape(1, -1).astype(jnp.float32)

    wlat, wbig, wuq, wkn, wv = _pack_weights(w_in, w_uq, w_ukv)
    ws = w_spatial.astype(jnp.float32)
    bs = jnp.broadcast_to(b_spatial[:, :, None], (GROUPS, SBLOCK, GDIM)).astype(jnp.float32)
    consts = (row(norm_in_g), wlat, wbig, row(q_norm_g), wuq, row(kv_norm_g), wkn, wv,
              row(gmlp_ln_g), row(gmlp_ln_b), ws, bs, row(out_norm_gmlp_g))

    qt, k, vt, gate, og = _proj_call(x2, _rope_tables(seq), consts, batch, seq)
    o_mla = _attn_call(qt, k, vt, batch, seq)
    out = _out_call(x2, o_mla, gate, og, row(out_norm_mla_g), w_out.astype(jnp.bfloat16),
                    row(final_norm_g))
    return out.reshape(batch, seq, d)
```

```python
import functools
import math

import jax
import jax.numpy as jnp
from jax import lax
from jax.experimental import pallas as pl
from jax.experimental.pallas import tpu as pltpu

D_MODEL = 1024
CHUNK = 64
EPS = 1e-6

HEADS = 8
NOPE = 64
ROPE = 32
DV = 128
DQK = 128
Q_RANK = 384
KV_RANK = 256
MLA_W = HEADS * DV
ROPE_THETA = 10000.0
QK_SCALE = math.log2(math.e) / math.sqrt(NOPE + ROPE)

GROUPS = 8
GDIM = 128
GMLP_W = GROUPS * GDIM
SBLOCK = 128

L_QLAT = 0
L_KVLAT = L_QLAT + Q_RANK
L_KR = L_KVLAT + KV_RANK
D_LAT = L_KR + DQK
B_GMLA = 0
B_U = B_GMLA + MLA_W
B_V = B_U + GMLP_W
B_GG = B_V + GMLP_W
D_BIG = B_GG + GMLP_W

VMEM_LIMIT = 56 * 1024 * 1024
ATTN_VMEM_LIMIT = 60 * 1024 * 1024

TM_PROJ = 512
SUB_PROJ = 256
TQ = 512
TK = 512
HB = 8
KV_CHUNKS = 4
LOOP_TILES = 4
LB = 128
TM_OUT = 1024
PACK_COLS = 1024
POS_SPLIT = 128
SUB_OUT = 256

NEG_BIG = -1e30


def _rms(x, g, eps=EPS):
    return x * lax.rsqrt(jnp.mean(x * x, axis=-1, keepdims=True) + eps) * g


def _dot(a, b):
    return jnp.dot(a, b, preferred_element_type=jnp.float32)


def _silu(x):
    return x * (1.0 / (1.0 + jnp.exp2(x * (-math.log2(math.e)))))


def _gelu_x2(x):
    return x * (1.0 + lax.erf(x * (1.0 / math.sqrt(2.0))))


def _rope(x, cos, sin):
    return x * cos + pltpu.roll(x, DQK - ROPE, 1) * sin


def _proj_kernel(x_ref, cost_ref, sint_ref, ng_ref, wlat_ref, wbig_ref,
                 qg_ref, wuqt_ref, kvg_ref, wkn_ref, wvt_ref, lng_ref, lnb_ref, ws_ref, bs_ref,
                 ogg_ref, qt_ref, k_ref, vt_ref, gate_ref, og_ref):
    tc = lax.broadcasted_iota(jnp.int32, (SBLOCK, SBLOCK), 0) // CHUNK
    sc = lax.broadcasted_iota(jnp.int32, (SBLOCK, SBLOCK), 1) // CHUNK
    ws = [jnp.where(sc <= tc, ws_ref[g], 0.0).astype(jnp.bfloat16) for g in range(GROUPS)]
    nblk = SUB_PROJ // SBLOCK

    for r0 in range(0, TM_PROJ, SUB_PROJ):
        rows = slice(r0, r0 + SUB_PROJ)
        y = _rms(x_ref[rows, :], ng_ref[...]).astype(jnp.bfloat16)

        v = _gelu_x2(_dot(y, wbig_ref[:, B_V:B_GG]))
        u = _gelu_x2(_dot(y, wbig_ref[:, B_U:B_V]))
        mu = jnp.mean(v, axis=-1, keepdims=True)
        vc = v - mu
        var = jnp.mean(vc * vc, axis=-1, keepdims=True)
        vln = (vc * lax.rsqrt(var + 4.0 * EPS) * lng_ref[...] + lnb_ref[...]).astype(jnp.bfloat16)

        lat = _dot(y, wlat_ref[...])
        gate_ref[rows, :] = _silu(_dot(y, wbig_ref[:, B_GMLA:B_U])).astype(jnp.bfloat16)
        ggs = _silu(_dot(y, wbig_ref[:, B_GG:D_BIG]))

        mixed_g = []
        for g in range(GROUPS):
            cols = slice(g * GDIM, (g + 1) * GDIM)
            vb = jnp.concatenate([vln[c * SBLOCK:(c + 1) * SBLOCK, cols] for c in range(nblk)],
                                 axis=1)
            mixed_g.append(_dot(ws[g], vb))
        mixed = jnp.concatenate(
            [jnp.concatenate([mixed_g[g][:, c * GDIM:(c + 1) * GDIM] + bs_ref[g]
                              for g in range(GROUPS)], axis=1) for c in range(nblk)], axis=0)
        og_ref[rows, :] = (_rms(u * mixed, ogg_ref[...], 4.0 * EPS) * ggs).astype(jnp.bfloat16)

        qn = _rms(lat[:, L_QLAT:L_KVLAT], qg_ref[...])
        kvn = _rms(lat[:, L_KVLAT:L_KR], kvg_ref[...])
        cos_t = cost_ref[:, rows]
        sin_t = sint_ref[:, rows]
        kr = _rope(lat[:, L_KR:D_LAT], cos_t.T, sin_t.T)
        qt_all = _dot(wuqt_ref[...], qn.T.astype(jnp.bfloat16))
        k_all = _dot(kvn.astype(jnp.bfloat16), wkn_ref[...])
        vt_all = _dot(wvt_ref[...], kvn.T.astype(jnp.bfloat16))
        cosq = cos_t * QK_SCALE
        sinq = sin_t * QK_SCALE
        heads = [slice(h * DQK, (h + 1) * DQK) for h in range(HEADS)]
        for h, sl in enumerate(heads):
            qb = qt_all[sl, :]
            rot = jnp.concatenate([qb[:NOPE], qb[NOPE + ROPE:], qb[NOPE + ROPE:]], axis=0)
            qt_ref[0, h, :, rows] = (qb * cosq + rot * sinq).astype(jnp.bfloat16)
        for h, sl in enumerate(heads):
            k_ref[0, h, rows, :] = (k_all[:, sl] + kr).astype(jnp.bfloat16)
        for h, sl in enumerate(heads):
            vt_ref[0, h, :, rows] = vt_all[sl, :].astype(jnp.bfloat16)


def _const_spec(shape):
    nd = len(shape)
    return pl.BlockSpec(shape, lambda i, _nd=nd: (0,) * _nd, pipeline_mode=pl.Buffered(1))


def _proj_call(x2, tables, consts, batch, seq):
    n = x2.shape[0]
    tm = TM_PROJ
    tiles_per_seq = seq // tm
    row_spec = pl.BlockSpec((tm, D_MODEL), lambda i: (i, 0))
    tab_spec = pl.BlockSpec((DQK, tm), lambda i: (0, i % tiles_per_seq))
    head_spec = pl.BlockSpec((1, HEADS, tm, DQK),
                             lambda i: (i // tiles_per_seq, 0, i % tiles_per_seq, 0))
    head_shape = jax.ShapeDtypeStruct((batch, HEADS, seq, DQK), jnp.bfloat16)
    headt_spec = pl.BlockSpec((1, HEADS, DQK, tm),
                              lambda i: (i // tiles_per_seq, 0, 0, i % tiles_per_seq))
    headt_shape = jax.ShapeDtypeStruct((batch, HEADS, DQK, seq), jnp.bfloat16)
    row_shape = jax.ShapeDtypeStruct((n, D_MODEL), jnp.bfloat16)
    return pl.pallas_call(
        _proj_kernel,
        grid=(n // tm,),
        in_specs=[row_spec] + [tab_spec] * len(tables) + [_const_spec(c.shape) for c in consts],
        out_specs=[headt_spec, head_spec, headt_spec, row_spec, row_spec],
        out_shape=[headt_shape, head_shape, headt_shape, row_shape, row_shape],
        compiler_params=pltpu.CompilerParams(
            dimension_semantics=("parallel",), vmem_limit_bytes=VMEM_LIMIT),
        name="proj_gmlp",
    )(x2, *tables, *consts)


def _attn_kernel(qt_ref, qtn_ref, k_hbm, vt_hbm, omg_ref, gate_ref, o_ref, k_ref, vt_ref, ksem, vsem,
                 s_ref, cm_ref, acc_ref, *, nb, nq):
    b = pl.program_id(0)
    qi = pl.program_id(1)
    chunk = k_ref.shape[1] // KV_CHUNKS
    assert chunk == LOOP_TILES * TK and (nq - 1) // LOOP_TILES >= 2
    more_batches = b < nb - 1
    acc_ref[...] = jnp.zeros_like(acc_ref)

    def k_copy(bb, c):
        rows = pl.ds(c * chunk, chunk)
        return pltpu.make_async_copy(k_hbm.at[bb, :, rows, :], k_ref.at[:, rows, :], ksem.at[c])

    def v_copy(bb, c):
        cols = pl.ds(c * chunk, chunk)
        return pltpu.make_async_copy(vt_hbm.at[bb, :, :, cols], vt_ref.at[:, :, cols],
                                     vsem.at[c])

    @pl.when(qi == 0)
    def _():
        @pl.when(b == 0)
        def _():
            k_copy(0, 0).start()
            v_copy(0, 0).start()
            k_copy(0, 0).wait()
        v_copy(b, 0).wait()
        for c in range(1, KV_CHUNKS):
            k_copy(b, c).start()
            v_copy(b, c).start()

    for c in range(1, KV_CHUNKS):
        @pl.when(qi == c * (chunk // TK))
        def _(c=c):
            k_copy(b, c).wait()
            v_copy(b, c).wait()

    def scores(q_ref, t, j):
        start = pl.multiple_of(t * TK, TK)
        return _dot(k_ref[j, pl.ds(start, TK), :], q_ref[0, j])

    def park(j, st):
        s_ref[j] = st
        cm_ref[j] = jnp.max(st, axis=0, keepdims=True)

    def consume(t, j, st, state, cmax):
        m_prev, l_prev = state
        start = pl.multiple_of(t * TK, TK)
        m_new = jnp.maximum(m_prev, cmax)
        alpha = jnp.exp2(m_prev - m_new)
        pt = jnp.exp2(st - m_new)
        l_new = alpha * l_prev + jnp.sum(pt, axis=0, keepdims=True)
        vt = vt_ref[j, :, pl.ds(start, TK)]
        return m_new, l_new, alpha, (vt, pt.astype(jnp.bfloat16))

    @pl.when((qi == 0) & (b == 0))
    def _():
        for j in range(HB):
            park(j, scores(qt_ref, 0, j))

    def body(t, carry):
        out = []
        for j in range(HB):
            m_prev, l_prev = carry[j]
            m_new, l_new, alpha, (vt, pt) = consume(t, j, s_ref[j], (m_prev, l_prev), cm_ref[j])
            park(j, scores(qt_ref, t + 1, j))
            acc_ref[j] = alpha * acc_ref[j] + _dot(vt, pt)
            out.append((m_new, l_new))
        return tuple(out)

    def diagonal(carry):
        kc = lax.broadcasted_iota(jnp.int32, (LB, LB), 0) // CHUNK
        qc = lax.broadcasted_iota(jnp.int32, (LB, LB), 1) // CHUNK
        diag_ok = kc <= qc
        start = pl.multiple_of(qi * TK, TK)
        sumsq = jnp.zeros((1, TQ), jnp.float32)
        for j in range(HB):
            m_prev, l_prev = carry[j]
            p_blocks, m_blocks, l_blocks = [], [], []
            for c in range(TQ // LB):
                lanes = slice(c * LB, (c + 1) * LB)
                sq = jnp.where(diag_ok, s_ref[j, c * LB:(c + 1) * LB, lanes], NEG_BIG)
                if c:
                    st = jnp.concatenate([s_ref[j, 0:c * LB, lanes], sq], axis=0)
                else:
                    st = sq
                m_new = jnp.maximum(m_prev[:, lanes], jnp.max(st, axis=0, keepdims=True))
                pt = jnp.exp2(st - m_new)
                m_blocks.append(m_new)
                l_blocks.append(jnp.sum(pt, axis=0, keepdims=True))
                pt = pt.astype(jnp.bfloat16)
                if (c + 1) * LB < TK:
                    pt = jnp.concatenate(
                        [pt, jnp.zeros((TK - (c + 1) * LB, LB), jnp.bfloat16)], axis=0)
                p_blocks.append(pt)
            m_new = jnp.concatenate(m_blocks, axis=1)
            alpha = jnp.exp2(m_prev - m_new)
            l_new = alpha * l_prev + jnp.concatenate(l_blocks, axis=1)
            park(j, scores(qtn_ref, 0, j))
            pv = _dot(vt_ref[j, :, pl.ds(start, TK)], jnp.concatenate(p_blocks, axis=1))
            ot = (alpha * acc_ref[j] + pv) * (1.0 / l_new)
            acc_ref[j] = ot
            sumsq = sumsq + jnp.sum(ot * ot, axis=0, keepdims=True)
        rinv = lax.rsqrt(sumsq * (1.0 / MLA_W) + EPS)
        for j in range(HB):
            cols = slice(j * DV, (j + 1) * DV)
            o = (acc_ref[j] * rinv).T * omg_ref[:, cols] * gate_ref[:, cols].astype(jnp.float32)
            o_ref[:, cols] = o.astype(o_ref.dtype)

    init = tuple((jnp.full((1, TQ), NEG_BIG, jnp.float32), jnp.zeros((1, TQ), jnp.float32))
                 for _ in range(HB))
    def tiles(first, count, c):
        for i in range(count):
            c = body(first + i, c)
        return c

    def loop_step(u, c):
        @pl.when((qi == nq - 1) & more_batches & (u == 1))
        def _():
            k_copy(b + 1, 0).start()
            v_copy(b + 1, 0).start()

        return tiles(LOOP_TILES * u, LOOP_TILES, c)

    carry = lax.fori_loop(0, qi // LOOP_TILES, loop_step, init)

    @pl.when((qi == nq - 1) & more_batches)
    def _():
        k_copy(b + 1, 0).wait()

    assert LOOP_TILES == 4
    rem = qi % LOOP_TILES

    def finish(count, c):
        diagonal(tiles(qi - rem, count, c))
        return 0

    lax.cond(rem >= 2,
             lambda c: lax.cond(rem == 3, functools.partial(finish, 3),
                                functools.partial(finish, 2), c),
             lambda c: lax.cond(rem == 1, functools.partial(finish, 1),
                                functools.partial(finish, 0), c),
             carry)


def _attn_call(qt, k, vt, omg, gate, batch, seq):
    assert TQ == TK and HB == HEADS
    nq = seq // TQ

    def next_tile(b, i):
        last = i == nq - 1
        return (jnp.where(last, jnp.minimum(b + 1, batch - 1), b), 0, 0,
                jnp.where(last, 0, i + 1))

    return pl.pallas_call(
        functools.partial(_attn_kernel, nb=batch, nq=nq),
        grid=(batch, nq),
        in_specs=[
            pl.BlockSpec((1, HB, DQK, TQ), lambda b, i: (b, 0, 0, i)),
            pl.BlockSpec((1, HB, DQK, TQ), next_tile),
            pl.BlockSpec(memory_space=pl.ANY),
            pl.BlockSpec(memory_space=pl.ANY),
            pl.BlockSpec((1, MLA_W), lambda b, i: (0, 0)),
            pl.BlockSpec((TQ, MLA_W), lambda b, i: (b * nq + i, 0)),
        ],
        out_specs=pl.BlockSpec((TQ, HB * DV), lambda b, i: (b * nq + i, 0)),
        out_shape=jax.ShapeDtypeStruct((batch * seq, MLA_W), jnp.bfloat16),
        scratch_shapes=[pltpu.VMEM((HB, seq, DQK), jnp.bfloat16),
                        pltpu.VMEM((HB, DV, seq), jnp.bfloat16),
                        pltpu.SemaphoreType.DMA((KV_CHUNKS,)),
                        pltpu.SemaphoreType.DMA((KV_CHUNKS,)),
                        pltpu.VMEM((HB, TK, TQ), jnp.float32),
                        pltpu.VMEM((HB, 1, TQ), jnp.float32),
                        pltpu.VMEM((HB, DV, TQ), jnp.float32)],
        compiler_params=pltpu.CompilerParams(
            dimension_semantics=("arbitrary", "arbitrary"),
            vmem_limit_bytes=ATTN_VMEM_LIMIT),
        name="mla_attn",
    )(qt, qt, k, vt, omg, gate)


def _out_kernel(x_ref, om_ref, og_ref, wo_ref, fg_ref, out_ref):
    for r0 in range(0, TM_OUT, SUB_OUT):
        rows = slice(r0, r0 + SUB_OUT)
        h = x_ref[rows, :] + _dot(og_ref[rows, :], wo_ref[MLA_W:, :])
        h = h + _dot(om_ref[rows, :], wo_ref[0:MLA_W, :])
        out_ref[rows, :] = _rms(h, fg_ref[...])


def _out_call(x2, om, og, wo, fg):
    n = x2.shape[0]
    tm = TM_OUT
    row_spec = pl.BlockSpec((tm, D_MODEL), lambda i: (i, 0))
    return pl.pallas_call(
        _out_kernel,
        grid=(n // tm,),
        in_specs=[row_spec, row_spec, row_spec, _const_spec(wo.shape), _const_spec(fg.shape)],
        out_specs=row_spec,
        out_shape=jax.ShapeDtypeStruct((n, D_MODEL), jnp.float32),
        compiler_params=pltpu.CompilerParams(
            dimension_semantics=("parallel",), vmem_limit_bytes=VMEM_LIMIT),
        name="out_proj",
    )(x2, om, og, wo, fg)


def _rope_tables(seq):
    inv_freq = ROPE_THETA ** (-jnp.arange(0, ROPE, 2, dtype=jnp.float32) / ROPE)
    hi = jnp.arange(0, seq, POS_SPLIT, dtype=jnp.float32)[:, None] * inv_freq[None, :]
    lo = jnp.arange(POS_SPLIT, dtype=jnp.float32)[:, None] * inv_freq[None, :]
    ch, sh = jnp.cos(hi)[:, None, :], jnp.sin(hi)[:, None, :]
    cl, sl = jnp.cos(lo)[None, :, :], jnp.sin(lo)[None, :, :]
    cos = (ch * cl - sh * sl).reshape(seq, ROPE // 2)
    sin = (sh * cl + ch * sl).reshape(seq, ROPE // 2)
    ones = jnp.ones((NOPE, seq), jnp.float32)
    zeros_n = jnp.zeros((NOPE, seq), jnp.float32)
    pad = jnp.zeros((DQK - NOPE - ROPE, seq), jnp.float32)
    cos_t = jnp.concatenate([ones, cos.T, cos.T, pad], axis=0)
    sin_t = jnp.concatenate([zeros_n, sin.T, sin.T, pad], axis=0)
    return cos_t, sin_t


def _rot_half_cols(w):
    half = ROPE // 2
    return jnp.concatenate([-w[..., half:], w[..., :half]], axis=-1)


def _pack_big_kernel(wt_ref, wbig_ref):
    wbig_ref[...] = wt_ref[...].T.astype(jnp.bfloat16)


def _pack_lat_kernel(wt_ref, wlat_ref):
    w = wt_ref[...]
    kr = w[L_KR:L_KR + ROPE]
    half = ROPE // 2
    blk = jnp.concatenate([w[:L_KR], jnp.zeros((NOPE, w.shape[1]), w.dtype), kr,
                           -kr[half:], kr[:half]], axis=0)
    wlat_ref[...] = blk.T.astype(jnp.bfloat16)


def _pack_w_in(w_in):
    d, d_in = w_in.shape
    wt = w_in.T
    n_lat = L_KR + ROPE
    wbig = pl.pallas_call(
        _pack_big_kernel,
        grid=(D_BIG // PACK_COLS,),
        in_specs=[pl.BlockSpec((pl.Element(PACK_COLS), pl.Element(d)),
                               lambda j: (pl.multiple_of(n_lat + j * PACK_COLS, ROPE), 0))],
        out_specs=pl.BlockSpec((d, PACK_COLS), lambda j: (0, j)),
        out_shape=jax.ShapeDtypeStruct((d, D_BIG), jnp.bfloat16),
        compiler_params=pltpu.CompilerParams(dimension_semantics=("parallel",)),
        name="pack_w_big",
    )(wt)
    wlat = pl.pallas_call(
        _pack_lat_kernel,
        grid=(1,),
        in_specs=[pl.BlockSpec((n_lat, d), lambda j: (0, 0))],
        out_specs=pl.BlockSpec((d, D_LAT), lambda j: (0, 0)),
        out_shape=jax.ShapeDtypeStruct((d, D_LAT), jnp.bfloat16),
        name="pack_w_lat",
    )(wt)
    return wlat, wbig


def _pack_weights(w_in, w_uq, w_ukv):
    bf = jnp.bfloat16
    wlat, wbig = _pack_w_in(w_in)
    wq = w_uq.reshape(Q_RANK, HEADS, NOPE + ROPE)
    wuq = jnp.concatenate([wq, _rot_half_cols(wq[..., NOPE:])],
                          axis=-1).reshape(Q_RANK, HEADS * DQK).T.astype(bf)
    wkv = w_ukv.reshape(KV_RANK, HEADS, NOPE + DV)
    wkn = jnp.concatenate([wkv[..., :NOPE], jnp.zeros((KV_RANK, HEADS, DQK - NOPE), w_ukv.dtype)],
                          axis=-1).reshape(KV_RANK, HEADS * DQK).astype(bf)
    wv = wkv[..., NOPE:].reshape(KV_RANK, HEADS * DV).T.astype(bf)
    return wlat, wbig, wuq, wkn, wv


def kernel(x, norm_in_g, w_in, q_norm_g, w_uq, kv_norm_g, w_ukv, gmlp_ln_g, gmlp_ln_b,
           w_spatial, b_spatial, out_norm_mla_g, out_norm_gmlp_g, w_out, final_norm_g):
    batch, seq, d = x.shape
    n = batch * seq
    x2 = x.reshape(n, d)
    row = lambda a: a.reshape(1, -1).astype(jnp.float32)

    wlat, wbig, wuq, wkn, wv = _pack_weights(w_in, w_uq, w_ukv)
    ws = w_spatial.astype(jnp.float32)
    bs = jnp.broadcast_to(b_spatial[:, :, None], (GROUPS, SBLOCK, GDIM)).astype(jnp.float32)
    consts = (row(norm_in_g), wlat, wbig, row(q_norm_g), wuq, row(kv_norm_g), wkn, wv,
              row(gmlp_ln_g), row(gmlp_ln_b), ws, bs, row(out_norm_gmlp_g))

    qt, k, vt, gate, og = _proj_call(x2, _rope_tables(seq), consts, batch, seq)
    om = _attn_call(qt, k, vt, row(out_norm_mla_g), gate, batch, seq)
    out = _out_call(x2, om, og, w_out.astype(jnp.bfloat16), row(final_norm_g))
    return out.reshape(batch, seq, d)
```
